```python
import jax, jax.numpy as jnp
from jax import lax
import numpy as np

D_MODEL = 1024
BATCH = 4
SEQ = 8192
DEPTH = 2

CHUNK = 64
N_EVEN = (DEPTH + 1) // 2
N_ODD = DEPTH // 2
A_WIDTH = D_MODEL // 2
A_GROUPS = 8
A_WIDTH_CONV = 31
B_WIDTH = D_MODEL // 2
B_GROUPS = 8
B_WIDTH_CONV = 3
IN_EVEN = 2 * A_WIDTH + 3 * B_WIDTH
MIX_EVEN = A_WIDTH + B_WIDTH
C_WIDTH = D_MODEL
C_GROUPS = 8
C_GROUP_DIM = C_WIDTH // C_GROUPS
C_BLOCK = 128
MEM_LEN = 256
XA_HEADS = 4
XA_HEAD_DIM = D_MODEL // XA_HEADS
D_FF = 2816
N_EXPERTS = 8
TOP_K = 2
RMS_EPS = 1e-6
LN_EPS = 1e-5

kernel_name = "hybrid_conv_gmlp_moe_encoder"


def rms_norm(x, g):
    xf = x.astype(jnp.float32)
    y = xf * lax.rsqrt(jnp.mean(xf * xf, axis=-1, keepdims=True) + RMS_EPS)
    return (y * g.astype(jnp.float32)).astype(x.dtype)


def layer_norm(x, g, b):
    xf = x.astype(jnp.float32)
    mu = jnp.mean(xf, axis=-1, keepdims=True)
    var = jnp.mean(jnp.square(xf - mu), axis=-1, keepdims=True)
    y = (xf - mu) * lax.rsqrt(var + LN_EPS)
    return (y * g.astype(jnp.float32) + b.astype(jnp.float32)).astype(x.dtype)


def causal_depthwise_conv(x, w):
    k_len, ch = w.shape
    return lax.conv_general_dilated(
        x, w[:, None, :].astype(x.dtype), window_strides=(1,), padding=[(k_len - 1, 0)],
        dimension_numbers=("NWC", "WIO", "NWC"), feature_group_count=ch)


def swiglu(h, w_gate, w_up, w_down):
    return (jax.nn.silu(h @ w_gate) * (h @ w_up)) @ w_down


def conv_mixers(h, w_in, a_conv_w, a_conv_b, a_ln_g, a_ln_b, b_conv_w, w_out):
    z = h @ w_in
    a_in, b_in = z[..., :2 * A_WIDTH], z[..., 2 * A_WIDTH:]
    a_val, a_gate = jnp.split(a_in, 2, axis=-1)
    a = a_val * jax.nn.sigmoid(a_gate)
    a = causal_depthwise_conv(a, a_conv_w) + a_conv_b.astype(a.dtype)
    a = jax.nn.silu(layer_norm(a, a_ln_g, a_ln_b))
    g_b, g_c, hb = jnp.split(b_in, 3, axis=-1)
    bo = g_b * causal_depthwise_conv(g_c * hb, b_conv_w)
    return jnp.concatenate([a, bo], axis=-1) @ w_out


def spatial_gating_mixer(h, w_in, ln_g, ln_b, w_s, b_s, w_out):
    bsz, seq, _ = h.shape
    z = jax.nn.gelu(h @ w_in)
    u, v = jnp.split(z, 2, axis=-1)
    v = layer_norm(v, ln_g, ln_b)
    v = v.reshape(bsz, seq // C_BLOCK, C_BLOCK, C_GROUPS, C_GROUP_DIM)
    pos = jnp.arange(C_BLOCK)
    mask = (pos[None, :] // CHUNK) <= (pos[:, None] // CHUNK)
    ws = jnp.where(mask[None], w_s, jnp.zeros_like(w_s)).astype(v.dtype)
    s = jnp.einsum("gij,bnjgc->bnigc", ws, v) + b_s.T[:, :, None].astype(v.dtype)
    s = s.reshape(bsz, seq, C_WIDTH)
    return (u * s) @ w_out


def memory_cross_attention(h, mem_n, w_q, w_k, w_v, w_o):
    bsz, seq, _ = h.shape
    m = mem_n.shape[1]
    q = (h @ w_q).reshape(bsz, seq, XA_HEADS, XA_HEAD_DIM)
    k = (mem_n @ w_k).reshape(bsz, m, XA_HEADS, XA_HEAD_DIM)
    v = (mem_n @ w_v).reshape(bsz, m, XA_HEADS, XA_HEAD_DIM)
    s = jnp.einsum("bshd,bmhd->bhsm", q, k).astype(jnp.float32) * (XA_HEAD_DIM ** -0.5)
    p = jax.nn.softmax(s, axis=-1).astype(v.dtype)
    o = jnp.einsum("bhsm,bmhd->bshd", p, v).reshape(bsz, seq, XA_HEADS * XA_HEAD_DIM)
    return o @ w_o


def moe_swiglu(h, w_router, w_gate, w_up, w_down):
    bsz, seq, d = h.shape
    t = h.reshape(bsz * seq, d)
    logits = (t @ w_router).astype(jnp.float32)
    top_v, top_i = lax.top_k(logits, TOP_K)
    top_w = jax.nn.softmax(top_v, axis=-1)
    gates = jnp.sum(jax.nn.one_hot(top_i, N_EXPERTS, dtype=jnp.float32) * top_w[..., None], axis=1)
    gates = gates.astype(t.dtype)
    out = jnp.zeros_like(t)
    for e in range(N_EXPERTS):
        out = out + gates[:, e:e + 1] * swiglu(t, w_gate[e], w_up[e], w_down[e])
    return out.reshape(bsz, seq, d)


def setup_inputs(seed: int = 0) -> dict:
    key = jax.random.key(seed)
    ks = iter(jax.random.split(key, 40))
    f32 = jnp.float32

    def nrm(shape, scale):
        return jax.random.normal(next(ks), shape, f32) * scale

    def gain(shape):
        return 1.0 + 0.05 * jax.random.normal(next(ks), shape, f32)

    d = D_MODEL
    return {
        "x": nrm((BATCH, SEQ, d), 1.0),
        "mem": nrm((BATCH, MEM_LEN, d), 1.0),
        "norm_mix_g": gain((DEPTH, d)),
        "norm_xattn_g": gain((DEPTH, d)),
        "norm_mem_g": gain((DEPTH, d)),
        "norm_ffn_g": gain((DEPTH, d)),
        "final_norm_g": gain((d,)),
        "xa_w_q": nrm((DEPTH, d, d), d ** -0.5),
        "xa_w_k": nrm((DEPTH, d, d), d ** -0.5),
        "xa_w_v": nrm((DEPTH, d, d), d ** -0.5),
        "xa_w_o": nrm((DEPTH, d, d), d ** -0.5),
        "cv_w_in": nrm((N_EVEN, d, IN_EVEN), d ** -0.5),
        "cv_a_conv_w": nrm((N_EVEN, A_WIDTH_CONV, A_WIDTH), A_WIDTH_CONV ** -0.5),
        "cv_a_conv_b": nrm((N_EVEN, A_WIDTH), 0.02),
        "cv_a_ln_g": gain((N_EVEN, A_WIDTH)),
        "cv_a_ln_b": nrm((N_EVEN, A_WIDTH), 0.02),
        "cv_b_conv_w": nrm((N_EVEN, B_WIDTH_CONV, B_WIDTH), B_WIDTH_CONV ** -0.5),
        "cv_w_out": nrm((N_EVEN, MIX_EVEN, d), MIX_EVEN ** -0.5),
        "ffn_w_gate": nrm((N_EVEN, d, D_FF), d ** -0.5),
        "ffn_w_up": nrm((N_EVEN, d, D_FF), d ** -0.5),
        "ffn_w_down": nrm((N_EVEN, D_FF, d), D_FF ** -0.5),
        "sg_w_in": nrm((N_ODD, d, 2 * C_WIDTH), d ** -0.5),
        "sg_ln_g": gain((N_ODD, C_WIDTH)),
        "sg_ln_b": nrm((N_ODD, C_WIDTH), 0.02),
        "sg_w_s": nrm((N_ODD, C_GROUPS, C_BLOCK, C_BLOCK), C_BLOCK ** -0.5),
        "sg_b_s": gain((N_ODD, C_GROUPS, C_BLOCK)),
        "sg_w_out": nrm((N_ODD, C_WIDTH, d), C_WIDTH ** -0.5),
        "moe_w_router": nrm((N_ODD, d, N_EXPERTS), d ** -0.5),
        "moe_w_gate": nrm((N_ODD, N_EXPERTS, d, D_FF), d ** -0.5),
        "moe_w_up": nrm((N_ODD, N_EXPERTS, d, D_FF), d ** -0.5),
        "moe_w_down": nrm((N_ODD, N_EXPERTS, D_FF, d), D_FF ** -0.5),
    }


def reference(x, mem, norm_mix_g, norm_xattn_g, norm_mem_g, norm_ffn_g, final_norm_g,
              xa_w_q, xa_w_k, xa_w_v, xa_w_o,
              cv_w_in, cv_a_conv_w, cv_a_conv_b, cv_a_ln_g, cv_a_ln_b, cv_b_conv_w, cv_w_out,
              ffn_w_gate, ffn_w_up, ffn_w_down,
              sg_w_in, sg_ln_g, sg_ln_b, sg_w_s, sg_b_s, sg_w_out,
              moe_w_router, moe_w_gate, moe_w_up, moe_w_down):
    h = x
    for i in range(DEPTH):
        j = i // 2
        hn = rms_norm(h, norm_mix_g[i])
        if i % 2 == 0:
            h = h + conv_mixers(hn, cv_w_in[j], cv_a_conv_w[j], cv_a_conv_b[j], cv_a_ln_g[j],
                                cv_a_ln_b[j], cv_b_conv_w[j], cv_w_out[j])
        else:
            h = h + spatial_gating_mixer(hn, sg_w_in[j], sg_ln_g[j], sg_ln_b[j], sg_w_s[j],
                                         sg_b_s[j], sg_w_out[j])
        h = h + memory_cross_attention(rms_norm(h, norm_xattn_g[i]), rms_norm(mem, norm_mem_g[i]),
                                       xa_w_q[i], xa_w_k[i], xa_w_v[i], xa_w_o[i])
        hn = rms_norm(h, norm_ffn_g[i])
        if i % 2 == 0:
            h = h + swiglu(hn, ffn_w_gate[j], ffn_w_up[j], ffn_w_down[j])
        else:
            h = h + moe_swiglu(hn, moe_w_router[j], moe_w_gate[j], moe_w_up[j], moe_w_down[j])
    return rms_norm(h, final_norm_g)
```

```python
import functools

import jax
import jax.numpy as jnp
from jax import lax
from jax.experimental import pallas as pl
from jax.experimental.pallas import tpu as pltpu

F32 = jnp.float32
BF16 = jnp.bfloat16

D_MODEL = 1024
A_WIDTH = 512
A_TAPS = 31
B_WIDTH = 512
B_TAPS = 3
C_WIDTH = 1024
C_GROUPS = 8
C_GROUP_DIM = 128
C_BLOCK = 128
CHUNK = 64
XA_HEADS = 4
XA_HEAD_DIM = 256
D_FF = 2816
N_EXPERTS = 8
RMS_EPS = 1e-6
LN_EPS = 1e-5

V7X_LANES = 128
V7X_SUBLANES = 8
V7X_VMEM_LIMIT_BYTES = 60000 * 1024

TOKEN_TILE = 512
CONV_HALO = 32
CONV_ROW_CHUNK = 64
FF_CHUNKS = ((0, 512), (512, 512), (1024, 512), (1536, 512), (2048, 512), (2560, 256))


def _params(n_axes=1):
    return pltpu.CompilerParams(dimension_semantics=("arbitrary",) * n_axes,
                                vmem_limit_bytes=V7X_VMEM_LIMIT_BYTES)


def _resident(shape):
    return pl.BlockSpec(shape, lambda *_: (0,) * len(shape), pipeline_mode=pl.Buffered(1))


def _rms(x, g):
    return x * lax.rsqrt(jnp.mean(x * x, axis=-1, keepdims=True) + RMS_EPS) * g


def _layer_norm(x, g, b):
    mu = jnp.mean(x, axis=-1, keepdims=True)
    xc = x - mu
    var = jnp.mean(xc * xc, axis=-1, keepdims=True)
    return xc * lax.rsqrt(var + LN_EPS) * g + b


def _sigmoid(x):
    return 1.0 / (1.0 + jnp.exp(-x))


def _gelu_tanh(x):
    return 0.5 * x * (1.0 + jnp.tanh(0.7978845608028654 * (x + 0.044715 * (x * x * x))))


def _dot(a, b):
    return jnp.dot(a, b, preferred_element_type=F32)


def _conv_mixer_kernel(tiles_per_batch, x_ref, g_ref, w_in_ref, aw_ref, ab_ref, lng_ref, lnb_ref,
                       bw_ref, w_out_ref, o_ref, ext_ref, sh_ref, bext_ref, apost_ref):
    tm = x_ref.shape[0]

    @pl.when(pl.program_id(0) % tiles_per_batch == 0)
    def _():
        ext_ref[0:CONV_HALO, :] = jnp.zeros((CONV_HALO, A_WIDTH), F32)
        bext_ref[0:V7X_SUBLANES, :] = jnp.zeros((V7X_SUBLANES, B_WIDTH), F32)

    x = x_ref[...]
    hn = _rms(x, g_ref[...]).astype(BF16)

    a_val = _dot(hn, w_in_ref[:, 0:A_WIDTH])
    a_gate = _dot(hn, w_in_ref[:, A_WIDTH:2 * A_WIDTH])
    ext_ref[CONV_HALO:CONV_HALO + tm, :] = a_val * _sigmoid(a_gate)
    sh_rows = tm + CONV_HALO - V7X_SUBLANES
    for r in range(1, V7X_SUBLANES):
        sh_ref[r, 0:sh_rows, :] = ext_ref[r:r + sh_rows, :]

    first = CONV_HALO - (A_TAPS - 1)

    def conv_rows(c, carry):
        r0 = pl.multiple_of(c * CONV_ROW_CHUNK, CONV_ROW_CHUNK)
        acc = jnp.zeros((CONV_ROW_CHUNK, A_WIDTH), F32) + ab_ref[...]
        for k in range(A_TAPS):
            q, r = divmod(first + k, V7X_SUBLANES)
            start = pl.multiple_of(r0 + q * V7X_SUBLANES, V7X_SUBLANES)
            if r == 0:
                win = ext_ref[pl.ds(start, CONV_ROW_CHUNK), :]
            else:
                win = sh_ref[r, pl.ds(start, CONV_ROW_CHUNK), :]
            acc = acc + aw_ref[k:k + 1, :] * win
        y = _layer_norm(acc, lng_ref[...], lnb_ref[...])
        apost_ref[pl.ds(r0, CONV_ROW_CHUNK), :] = (y * _sigmoid(y)).astype(BF16)
        return carry

    lax.fori_loop(0, tm // CONV_ROW_CHUNK, conv_rows, 0)
    ext_ref[0:CONV_HALO, :] = ext_ref[tm:tm + CONV_HALO, :]

    base = 2 * A_WIDTH
    g_b = _dot(hn, w_in_ref[:, base:base + B_WIDTH])
    g_c = _dot(hn, w_in_ref[:, base + B_WIDTH:base + 2 * B_WIDTH])
    h_b = _dot(hn, w_in_ref[:, base + 2 * B_WIDTH:base + 3 * B_WIDTH])
    bext_ref[V7X_SUBLANES:V7X_SUBLANES + tm, :] = g_c * h_b
    conv_b = jnp.zeros((tm, B_WIDTH), F32)
    for k in range(B_TAPS):
        off = V7X_SUBLANES - (B_TAPS - 1) + k
        conv_b = conv_b + bw_ref[k:k + 1, :] * bext_ref[off:off + tm, :]
    bext_ref[0:V7X_SUBLANES, :] = bext_ref[tm:tm + V7X_SUBLANES, :]
    b_out = (g_b * conv_b).astype(BF16)

    o_ref[...] = (x + _dot(apost_ref[...], w_out_ref[0:A_WIDTH, :])
                  + _dot(b_out, w_out_ref[A_WIDTH:A_WIDTH + B_WIDTH, :]))


def _conv_mixer(x, g, w_in, aw, ab, lng, lnb, bw, w_out, seq):
    n, d = x.shape
    tm = TOKEN_TILE
    row = lambda i: (i, 0)
    return pl.pallas_call(
        functools.partial(_conv_mixer_kernel, seq // tm),
        grid=(n // tm,),
        in_specs=[pl.BlockSpec((tm, d), row), _resident(g.shape), _resident(w_in.shape),
                  _resident(aw.shape), _resident(ab.shape), _resident(lng.shape), _resident(lnb.shape),
                  _resident(bw.shape), _resident(w_out.shape)],
        out_specs=pl.BlockSpec((tm, d), row),
        out_shape=jax.ShapeDtypeStruct((n, d), F32),
        scratch_shapes=[pltpu.VMEM((tm + CONV_HALO, A_WIDTH), F32),
                        pltpu.VMEM((V7X_SUBLANES, tm + CONV_HALO, A_WIDTH), F32),
                        pltpu.VMEM((tm + V7X_SUBLANES, B_WIDTH), F32),
                        pltpu.VMEM((tm, A_WIDTH), BF16)],
        compiler_params=_params(),
        name="conv_mixer",
    )(x, g, w_in, aw, ab, lng, lnb, bw, w_out)


def _mem_kv_kernel(mem_ref, g_ref, wk_ref, wv_ref, kt_ref, v_ref):
    mn = _rms(mem_ref[...], g_ref[...]).astype(BF16)
    kt_ref[...] = _dot(mn, wk_ref[...]).T.astype(BF16)
    v_ref[...] = _dot(mn, wv_ref[...]).astype(BF16)


def _mem_kv(mem2d, g, wk, wv, batch, mem_len):
    d = mem2d.shape[1]
    return pl.pallas_call(
        _mem_kv_kernel,
        grid=(batch,),
        in_specs=[pl.BlockSpec((mem_len, d), lambda b: (b, 0)), _resident(g.shape),
                  _resident(wk.shape), _resident(wv.shape)],
        out_specs=[pl.BlockSpec((d, mem_len), lambda b: (b, 0)),
                   pl.BlockSpec((mem_len, d), lambda b: (b, 0))],
        out_shape=[jax.ShapeDtypeStruct((batch * d, mem_len), BF16),
                   jax.ShapeDtypeStruct((batch * mem_len, d), BF16)],
        compiler_params=_params(),
        name="mem_kv",
    )(mem2d, g, wk, wv)


def _attend(h, g_ref, wq_ref, kt_ref, v_ref, wo_ref):
    hn = _rms(h, g_ref[...]).astype(BF16)
    q = (_dot(hn, wq_ref[...]) * (XA_HEAD_DIM ** -0.5)).astype(BF16)
    heads = []
    for hd in range(XA_HEADS):
        lo, hi = hd * XA_HEAD_DIM, (hd + 1) * XA_HEAD_DIM
        s = _dot(q[:, lo:hi], kt_ref[lo:hi, :])
        p = jnp.exp(s - jnp.max(s, axis=-1, keepdims=True))
        p = p * (1.0 / jnp.sum(p, axis=-1, keepdims=True))
        heads.append(_dot(p.astype(BF16), v_ref[:, lo:hi]))
    o = jnp.concatenate(heads, axis=-1).astype(BF16)
    return h + _dot(o, wo_ref[...])


def _xattn_kernel(h_ref, g_ref, wq_ref, kt_ref, v_ref, wo_ref, o_ref):
    o_ref[...] = _attend(h_ref[...], g_ref, wq_ref, kt_ref, v_ref, wo_ref)


def _xattn_router_kernel(h_ref, g_ref, wq_ref, kt_ref, v_ref, wo_ref, gf_ref, wr_ref,
                         o_ref, hn_ref, gates_ref):
    h = _attend(h_ref[...], g_ref, wq_ref, kt_ref, v_ref, wo_ref)
    o_ref[...] = h
    hn = _rms(h, gf_ref[...])
    hn_ref[...] = hn.astype(BF16)
    logits = jnp.dot(hn, wr_ref[...], precision=lax.Precision.HIGHEST, preferred_element_type=F32)
    lane = lax.broadcasted_iota(jnp.int32, logits.shape, 1)
    neg = jnp.float32(-jnp.inf)
    logits = jnp.where(lane < N_EXPERTS, logits, neg)
    m1 = jnp.max(logits, axis=-1, keepdims=True)
    i1 = jnp.min(jnp.where(logits == m1, lane, V7X_LANES), axis=-1, keepdims=True)
    rest = jnp.where(lane == i1, neg, logits)
    m2 = jnp.max(rest, axis=-1, keepdims=True)
    i2 = jnp.min(jnp.where(rest == m2, lane, V7X_LANES), axis=-1, keepdims=True)
    e2 = jnp.exp(m2 - m1)
    inv = 1.0 / (1.0 + e2)
    gates_ref[...] = jnp.where(lane == i1, inv, jnp.where(lane == i2, e2 * inv, 0.0))


def _xattn(h, g, wq, kt, v, wo, seq, mem_len, router=None):
    n, d = h.shape
    tm = TOKEN_TILE
    tpb = seq // tm
    row = lambda i: (i, 0)
    in_specs = [pl.BlockSpec((tm, d), row), _resident(g.shape), _resident(wq.shape),
                pl.BlockSpec((d, mem_len), lambda i: (i // tpb, 0)),
                pl.BlockSpec((mem_len, d), lambda i: (i // tpb, 0)),
                _resident(wo.shape)]
    if router is None:
        return pl.pallas_call(
            _xattn_kernel, grid=(n // tm,), in_specs=in_specs,
            out_specs=pl.BlockSpec((tm, d), row),
            out_shape=jax.ShapeDtypeStruct((n, d), F32),
            compiler_params=_params(), name="xattn",
        )(h, g, wq, kt, v, wo)
    g_ffn, w_router = router
    return pl.pallas_call(
        _xattn_router_kernel, grid=(n // tm,),
        in_specs=in_specs + [_resident(g_ffn.shape), _resident(w_router.shape)],
        out_specs=[pl.BlockSpec((tm, d), row), pl.BlockSpec((tm, d), row),
                   pl.BlockSpec((tm, V7X_LANES), row)],
        out_shape=[jax.ShapeDtypeStruct((n, d), F32), jax.ShapeDtypeStruct((n, d), BF16),
                   jax.ShapeDtypeStruct((n, V7X_LANES), F32)],
        compiler_params=_params(), name="xattn_router",
    )(h, g, wq, kt, v, wo, g_ffn, w_router)


def _swiglu(hn, wg_ref, wu_ref, wd_ref):
    acc = None
    for start, size in FF_CHUNKS:
        gate = _dot(hn, wg_ref[:, start:start + size])
        up = _dot(hn, wu_ref[:, start:start + size])
        act = (gate * _sigmoid(gate) * up).astype(BF16)
        part = _dot(act, wd_ref[start:start + size, :])
        acc = part if acc is None else acc + part
    return acc


def _ffn_kernel(h_ref, g_ref, wg_ref, wu_ref, wd_ref, o_ref):
    h = h_ref[...]
    hn = _rms(h, g_ref[...]).astype(BF16)
    o_ref[...] = h + _swiglu(hn, wg_ref, wu_ref, wd_ref)


def _ffn(h, g, wg, wu, wd):
    n, d = h.shape
    tm = TOKEN_TILE
    row = lambda i: (i, 0)
    return pl.pallas_call(
        _ffn_kernel, grid=(n // tm,),
        in_specs=[pl.BlockSpec((tm, d), row), _resident(g.shape), _resident(wg.shape),
                  _resident(wu.shape), _resident(wd.shape)],
        out_specs=pl.BlockSpec((tm, d), row),
        out_shape=jax.ShapeDtypeStruct((n, d), F32),
        compiler_params=_params(), name="ffn",
    )(h, g, wg, wu, wd)


def _gmlp_kernel(h_ref, g_ref, w_in_ref, lng_ref, lnb_ref, ws_ref, bias_ref, w_out_ref, o_ref):
    tm = h_ref.shape[0]
    nblk = tm // C_BLOCK
    h = h_ref[...]
    hn = _rms(h, g_ref[...]).astype(BF16)
    u = _gelu_tanh(_dot(hn, w_in_ref[:, 0:C_WIDTH]))
    v = _gelu_tanh(_dot(hn, w_in_ref[:, C_WIDTH:2 * C_WIDTH]))
    v = _layer_norm(v, lng_ref[...], lnb_ref[...]).astype(BF16)

    qi = lax.broadcasted_iota(jnp.int32, (C_BLOCK, C_BLOCK), 0) // CHUNK
    kj = lax.broadcasted_iota(jnp.int32, (C_BLOCK, C_BLOCK), 1) // CHUNK
    causal = kj <= qi
    mixed = []
    for grp in range(C_GROUPS):
        lo, hi = grp * C_GROUP_DIM, (grp + 1) * C_GROUP_DIM
        w = jnp.where(causal, ws_ref[grp], 0.0).astype(BF16)
        rhs = jnp.concatenate([v[b * C_BLOCK:(b + 1) * C_BLOCK, lo:hi] for b in range(nblk)], axis=1)
        mixed.append(_dot(w, rhs))
    rows = []
    for b in range(nblk):
        blk = jnp.concatenate([m[:, b * C_GROUP_DIM:(b + 1) * C_GROUP_DIM] for m in mixed], axis=1)
        rows.append(blk + bias_ref[...])
    s = jnp.concatenate(rows, axis=0)
    o_ref[...] = h + _dot((u * s).astype(BF16), w_out_ref[...])


def _gmlp(h, g, w_in, lng, lnb, ws, bias, w_out):
    n, d = h.shape
    tm = TOKEN_TILE
    row = lambda i: (i, 0)
    return pl.pallas_call(
        _gmlp_kernel, grid=(n // tm,),
        in_specs=[pl.BlockSpec((tm, d), row), _resident(g.shape), _resident(w_in.shape),
                  _resident(lng.shape), _resident(lnb.shape), _resident(ws.shape),
                  _resident(bias.shape), _resident(w_out.shape)],
        out_specs=pl.BlockSpec((tm, d), row),
        out_shape=jax.ShapeDtypeStruct((n, d), F32),
        compiler_params=_params(), name="gmlp",
    )(h, g, w_in, lng, lnb, ws, bias, w_out)


def _moe_kernel(hn_ref, gates_ref, acc_ref, wg_ref, wu_ref, wd_ref, gf_ref, o_ref):
    e = pl.program_id(0)
    y = _swiglu(hn_ref[...], wg_ref, wu_ref, wd_ref)
    lane = lax.broadcasted_iota(jnp.int32, gates_ref.shape, 1)
    gate = jnp.sum(jnp.where(lane == e, gates_ref[...], 0.0), axis=-1, keepdims=True)
    acc = acc_ref[...] + gate * y

    @pl.when(e < N_EXPERTS - 1)
    def _():
        o_ref[...] = acc

    @pl.when(e == N_EXPERTS - 1)
    def _():
        o_ref[...] = _rms(acc, gf_ref[...])


def _moe(h, hn, gates, wg, wu, wd, g_final):
    n, d = h.shape
    tm = TOKEN_TILE
    row = lambda e, i: (i, 0)
    expert = lambda e, i: (e, 0, 0)
    return pl.pallas_call(
        _moe_kernel, grid=(N_EXPERTS, n // tm),
        in_specs=[pl.BlockSpec((tm, d), row), pl.BlockSpec((tm, V7X_LANES), row),
                  pl.BlockSpec((tm, d), row),
                  pl.BlockSpec((None, d, D_FF), expert), pl.BlockSpec((None, d, D_FF), expert),
                  pl.BlockSpec((None, D_FF, d), expert), _resident(g_final.shape)],
        out_specs=pl.BlockSpec((tm, d), row),
        out_shape=jax.ShapeDtypeStruct((n, d), F32),
        input_output_aliases={2: 0},
        compiler_params=_params(2), name="moe",
    )(hn, gates, h, wg, wu, wd, g_final)


def kernel(x, mem, norm_mix_g, norm_xattn_g, norm_mem_g, norm_ffn_g, final_norm_g, xa_w_q, xa_w_k, xa_w_v, xa_w_o, cv_w_in, cv_a_conv_w, cv_a_conv_b, cv_a_ln_g, cv_a_ln_b, cv_b_conv_w, cv_w_out, ffn_w_gate, ffn_w_up, ffn_w_down, sg_w_in, sg_ln_g, sg_ln_b, sg_w_s, sg_b_s, sg_w_out, moe_w_router, moe_w_gate, moe_w_up, moe_w_down):
    batch, seq, d = x.shape
    mem_len = mem.shape[1]
    n = batch * seq
    vec = lambda a: a.reshape(1, -1)
    bf = lambda a: a.astype(BF16)

    h = x.reshape(n, d)
    mem2d = mem.reshape(batch * mem_len, d)

    h = _conv_mixer(h, vec(norm_mix_g[0]), bf(cv_w_in[0]), cv_a_conv_w[0], vec(cv_a_conv_b[0]),
                    vec(cv_a_ln_g[0]), vec(cv_a_ln_b[0]), cv_b_conv_w[0], bf(cv_w_out[0]), seq)
    kt, v = _mem_kv(mem2d, vec(norm_mem_g[0]), bf(xa_w_k[0]), bf(xa_w_v[0]), batch, mem_len)
    h = _xattn(h, vec(norm_xattn_g[0]), bf(xa_w_q[0]), kt, v, bf(xa_w_o[0]), seq, mem_len)
    h = _ffn(h, vec(norm_ffn_g[0]), bf(ffn_w_gate[0]), bf(ffn_w_up[0]), bf(ffn_w_down[0]))

    bias = jnp.repeat(sg_b_s[0].T, C_GROUP_DIM, axis=1)
    h = _gmlp(h, vec(norm_mix_g[1]), bf(sg_w_in[0]), vec(sg_ln_g[0]), vec(sg_ln_b[0]), sg_w_s[0],
              bias, bf(sg_w_out[0]))
    kt, v = _mem_kv(mem2d, vec(norm_mem_g[1]), bf(xa_w_k[1]), bf(xa_w_v[1]), batch, mem_len)
    w_router = jnp.pad(moe_w_router[0], ((0, 0), (0, V7X_LANES - N_EXPERTS)))
    h, hn, gates = _xattn(h, vec(norm_xattn_g[1]), bf(xa_w_q[1]), kt, v, bf(xa_w_o[1]), seq, mem_len,
                          router=(vec(norm_ffn_g[1]), w_router))
    out = _moe(h, hn, gates, bf(moe_w_gate[0]), bf(moe_w_up[0]), bf(moe_w_down[0]), vec(final_norm_g))
    return out.reshape(batch, seq, d)
```

```python
import functools

import jax
import jax.numpy as jnp
from jax import lax
from jax.experimental import pallas as pl
from jax.experimental.pallas import tpu as pltpu

F32 = jnp.float32
BF16 = jnp.bfloat16
I32 = jnp.int32

D_MODEL = 1024
A_WIDTH = 512
A_TAPS = 31
B_WIDTH = 512
B_TAPS = 3
C_WIDTH = 1024
C_GROUPS = 8
C_GROUP_DIM = 128
C_BLOCK = 128
CHUNK = 64
XA_HEADS = 4
XA_HEAD_DIM = 256
D_FF = 2816
N_EXPERTS = 8
TOP_K = 2
RMS_EPS = 1e-6
LN_EPS = 1e-5

V7X_LANES = 128
V7X_SUBLANES = 8
V7X_VMEM_LIMIT_BYTES = 60000 * 1024

TOKEN_TILE = 512
CONV_HALO = 32
CONV_ROW_CHUNK = 64
FF_CHUNKS = ((0, 512), (512, 512), (1024, 512), (1536, 512), (2048, 512), (2560, 256))
ROW_CHUNKS = D_MODEL // V7X_LANES
assert ROW_CHUNKS == V7X_SUBLANES
ROUTE_W1, ROUTE_W2, ROUTE_E1, ROUTE_E2, ROUTE_R1, ROUTE_R2 = range(6)


def _params(n_axes=1):
    return pltpu.CompilerParams(dimension_semantics=("arbitrary",) * n_axes,
                                vmem_limit_bytes=V7X_VMEM_LIMIT_BYTES)


def _resident(shape):
    return pl.BlockSpec(shape, lambda *_: (0,) * len(shape), pipeline_mode=pl.Buffered(1))


def _rms(x, g):
    return x * lax.rsqrt(jnp.mean(x * x, axis=-1, keepdims=True) + RMS_EPS) * g


def _layer_norm(x, g, b):
    mu = jnp.mean(x, axis=-1, keepdims=True)
    xc = x - mu
    var = jnp.mean(xc * xc, axis=-1, keepdims=True)
    return xc * lax.rsqrt(var + LN_EPS) * g + b


def _sigmoid(x):
    return 1.0 / (1.0 + jnp.exp(-x))


def _gelu_tanh(x):
    return 0.5 * x * (1.0 + jnp.tanh(0.7978845608028654 * (x + 0.044715 * (x * x * x))))


def _dot(a, b):
    return jnp.dot(a, b, preferred_element_type=F32)


def _store_row_tiles(ref, x):
    rows = x.shape[0]
    for s in range(ROW_CHUNKS):
        ref[pl.ds(s, rows, stride=ROW_CHUNKS), :] = x[:, s * V7X_LANES:(s + 1) * V7X_LANES]


def _load_row_tiles(ref, base, rows):
    return jnp.concatenate(
        [ref[pl.ds(base + s, rows, stride=ROW_CHUNKS), :] for s in range(ROW_CHUNKS)], axis=1)


def _conv_mixer_kernel(tiles_per_batch, x_ref, g_ref, w_in_ref, aw_ref, ab_ref, lng_ref, lnb_ref,
                       bw_ref, w_out_ref, o_ref, ext_ref, sh_ref, bext_ref, apost_ref):
    tm = x_ref.shape[0]

    @pl.when(pl.program_id(0) % tiles_per_batch == 0)
    def _():
        ext_ref[0:CONV_HALO, :] = jnp.zeros((CONV_HALO, A_WIDTH), F32)
        bext_ref[0:V7X_SUBLANES, :] = jnp.zeros((V7X_SUBLANES, B_WIDTH), F32)

    x = x_ref[...]
    hn = _rms(x, g_ref[...]).astype(BF16)

    a_val = _dot(hn, w_in_ref[:, 0:A_WIDTH])
    a_gate = _dot(hn, w_in_ref[:, A_WIDTH:2 * A_WIDTH])
    ext_ref[CONV_HALO:CONV_HALO + tm, :] = a_val * _sigmoid(a_gate)
    sh_rows = tm + CONV_HALO - V7X_SUBLANES
    for r in range(1, V7X_SUBLANES):
        sh_ref[r, 0:sh_rows, :] = ext_ref[r:r + sh_rows, :]

    first = CONV_HALO - (A_TAPS - 1)

    def conv_rows(c, carry):
        r0 = pl.multiple_of(c * CONV_ROW_CHUNK, CONV_ROW_CHUNK)
        acc = jnp.zeros((CONV_ROW_CHUNK, A_WIDTH), F32) + ab_ref[...]
        for k in range(A_TAPS):
            q, r = divmod(first + k, V7X_SUBLANES)
            start = pl.multiple_of(r0 + q * V7X_SUBLANES, V7X_SUBLANES)
            if r == 0:
                win = ext_ref[pl.ds(start, CONV_ROW_CHUNK), :]
            else:
                win = sh_ref[r, pl.ds(start, CONV_ROW_CHUNK), :]
            acc = acc + aw_ref[k:k + 1, :] * win
        y = _layer_norm(acc, lng_ref[...], lnb_ref[...])
        apost_ref[pl.ds(r0, CONV_ROW_CHUNK), :] = (y * _sigmoid(y)).astype(BF16)
        return carry

    lax.fori_loop(0, tm // CONV_ROW_CHUNK, conv_rows, 0)
    ext_ref[0:CONV_HALO, :] = ext_ref[tm:tm + CONV_HALO, :]

    base = 2 * A_WIDTH
    g_b = _dot(hn, w_in_ref[:, base:base + B_WIDTH])
    g_c = _dot(hn, w_in_ref[:, base + B_WIDTH:base + 2 * B_WIDTH])
    h_b = _dot(hn, w_in_ref[:, base + 2 * B_WIDTH:base + 3 * B_WIDTH])
    bext_ref[V7X_SUBLANES:V7X_SUBLANES + tm, :] = g_c * h_b
    conv_b = jnp.zeros((tm, B_WIDTH), F32)
    for k in range(B_TAPS):
        off = V7X_SUBLANES - (B_TAPS - 1) + k
        conv_b = conv_b + bw_ref[k:k + 1, :] * bext_ref[off:off + tm, :]
    bext_ref[0:V7X_SUBLANES, :] = bext_ref[tm:tm + V7X_SUBLANES, :]
    b_out = (g_b * conv_b).astype(BF16)

    o_ref[...] = (x + _dot(apost_ref[...], w_out_ref[0:A_WIDTH, :])
                  + _dot(b_out, w_out_ref[A_WIDTH:A_WIDTH + B_WIDTH, :]))


def _conv_mixer(x, g, w_in, aw, ab, lng, lnb, bw, w_out, seq):
    n, d = x.shape
    tm = TOKEN_TILE
    row = lambda i: (i, 0)
    return pl.pallas_call(
        functools.partial(_conv_mixer_kernel, seq // tm),
        grid=(n // tm,),
        in_specs=[pl.BlockSpec((tm, d), row), _resident(g.shape), _resident(w_in.shape),
                  _resident(aw.shape), _resident(ab.shape), _resident(lng.shape), _resident(lnb.shape),
                  _resident(bw.shape), _resident(w_out.shape)],
        out_specs=pl.BlockSpec((tm, d), row),
        out_shape=jax.ShapeDtypeStruct((n, d), F32),
        scratch_shapes=[pltpu.VMEM((tm + CONV_HALO, A_WIDTH), F32),
                        pltpu.VMEM((V7X_SUBLANES, tm + CONV_HALO, A_WIDTH), F32),
                        pltpu.VMEM((tm + V7X_SUBLANES, B_WIDTH), F32),
                        pltpu.VMEM((tm, A_WIDTH), BF16)],
        compiler_params=_params(),
        name="conv_mixer",
    )(x, g, w_in, aw, ab, lng, lnb, bw, w_out)


def _mem_kv_kernel(mem_ref, g_ref, wk_ref, wv_ref, kt_ref, v_ref):
    mn = _rms(mem_ref[...], g_ref[...]).astype(BF16)
    kt_ref[...] = _dot(mn, wk_ref[...]).T.astype(BF16)
    v_ref[...] = _dot(mn, wv_ref[...]).astype(BF16)


def _mem_kv(mem2d, g, wk, wv, batch, mem_len):
    d = mem2d.shape[1]
    return pl.pallas_call(
        _mem_kv_kernel,
        grid=(batch,),
        in_specs=[pl.BlockSpec((mem_len, d), lambda b: (b, 0)), _resident(g.shape),
                  _resident(wk.shape), _resident(wv.shape)],
        out_specs=[pl.BlockSpec((d, mem_len), lambda b: (b, 0)),
                   pl.BlockSpec((mem_len, d), lambda b: (b, 0))],
        out_shape=[jax.ShapeDtypeStruct((batch * d, mem_len), BF16),
                   jax.ShapeDtypeStruct((batch * mem_len, d), BF16)],
        compiler_params=_params(),
        name="mem_kv",
    )(mem2d, g, wk, wv)


def _attend(h, g_ref, wq_ref, kt_ref, v_ref, wo_ref):
    hn = _rms(h, g_ref[...]).astype(BF16)
    q = (_dot(hn, wq_ref[...]) * (XA_HEAD_DIM ** -0.5)).astype(BF16)
    heads = []
    for hd in range(XA_HEADS):
        lo, hi = hd * XA_HEAD_DIM, (hd + 1) * XA_HEAD_DIM
        s = _dot(q[:, lo:hi], kt_ref[lo:hi, :])
        p = jnp.exp(s - jnp.max(s, axis=-1, keepdims=True))
        p = p * (1.0 / jnp.sum(p, axis=-1, keepdims=True))
        heads.append(_dot(p.astype(BF16), v_ref[:, lo:hi]))
    o = jnp.concatenate(heads, axis=-1).astype(BF16)
    return h + _dot(o, wo_ref[...])


def _xattn_kernel(h_ref, g_ref, wq_ref, kt_ref, v_ref, wo_ref, o_ref):
    o_ref[...] = _attend(h_ref[...], g_ref, wq_ref, kt_ref, v_ref, wo_ref)


def _xattn_router_kernel(h_ref, g_ref, wq_ref, kt_ref, v_ref, wo_ref, gf_ref, wr_ref, tri_ref,
                         o_ref, hn_ref, route_ref, counts_ref, seen_ref):
    @pl.when(pl.program_id(0) == 0)
    def _():
        seen_ref[...] = jnp.zeros(seen_ref.shape, F32)

    h = _attend(h_ref[...], g_ref, wq_ref, kt_ref, v_ref, wo_ref)
    o_ref[...] = h
    hn = _rms(h, gf_ref[...])
    _store_row_tiles(hn_ref, hn)

    hn_hi = hn.astype(BF16)
    hn_lo = (hn - hn_hi.astype(F32)).astype(BF16)
    p_hi = _dot(hn_hi, wr_ref[...])
    p_lo = _dot(hn_lo, wr_ref[...])
    logits = (pltpu.roll(p_hi, V7X_LANES - N_EXPERTS, axis=1) + p_lo) + p_hi

    lane = lax.broadcasted_iota(I32, logits.shape, 1)
    neg = jnp.float32(-jnp.inf)
    logits = jnp.where(lane < N_EXPERTS, logits, neg)
    m1 = jnp.max(logits, axis=-1, keepdims=True)
    e1 = jnp.min(jnp.where(logits == m1, lane, V7X_LANES), axis=-1, keepdims=True)
    rest = jnp.where(lane == e1, neg, logits)
    m2 = jnp.max(rest, axis=-1, keepdims=True)
    e2 = jnp.min(jnp.where(rest == m2, lane, V7X_LANES), axis=-1, keepdims=True)
    ex = jnp.exp(m2 - m1)
    w1 = 1.0 / (1.0 + ex)
    w2 = ex * w1

    chosen = (lane == e1) | (lane == e2)
    member = jnp.where(chosen, 1.0, 0.0)
    rank = _dot(tri_ref[...], member.astype(BF16)) + seen_ref[...]
    seen_ref[...] = seen_ref[...] + jnp.sum(member, axis=0, keepdims=True)
    counts_ref[...] = jnp.broadcast_to(seen_ref[...], counts_ref.shape)
    r1 = jnp.sum(jnp.where(lane == e1, rank, 0.0), axis=-1, keepdims=True)
    r2 = jnp.sum(jnp.where(lane == e2, rank, 0.0), axis=-1, keepdims=True)

    record = jnp.zeros(logits.shape, F32)
    for slot, val in ((ROUTE_W1, w1), (ROUTE_W2, w2), (ROUTE_E1, e1.astype(F32)),
                      (ROUTE_E2, e2.astype(F32)), (ROUTE_R1, r1), (ROUTE_R2, r2)):
        record = jnp.where(lane == slot, val, record)
    route_ref[...] = record


def _xattn(h, g, wq, kt, v, wo, seq, mem_len, router=None):
    n, d = h.shape
    tm = TOKEN_TILE
    tpb = seq // tm
    row = lambda i: (i, 0)
    in_specs = [pl.BlockSpec((tm, d), row), _resident(g.shape), _resident(wq.shape),
                pl.BlockSpec((d, mem_len), lambda i: (i // tpb, 0)),
                pl.BlockSpec((mem_len, d), lambda i: (i // tpb, 0)),
                _resident(wo.shape)]
    if router is None:
        return pl.pallas_call(
            _xattn_kernel, grid=(n // tm,), in_specs=in_specs,
            out_specs=pl.BlockSpec((tm, d), row),
            out_shape=jax.ShapeDtypeStruct((n, d), F32),
            compiler_params=_params(), name="xattn",
        )(h, g, wq, kt, v, wo)
    g_ffn, w_router, tri = router
    return pl.pallas_call(
        _xattn_router_kernel, grid=(n // tm,),
        in_specs=in_specs + [_resident(g_ffn.shape), _resident(w_router.shape), _resident(tri.shape)],
        out_specs=[pl.BlockSpec((tm, d), row), pl.BlockSpec((tm * ROW_CHUNKS, V7X_LANES), row),
                   pl.BlockSpec((tm, V7X_LANES), row),
                   pl.BlockSpec((V7X_SUBLANES, V7X_LANES), lambda i: (0, 0))],
        out_shape=[jax.ShapeDtypeStruct((n, d), F32),
                   jax.ShapeDtypeStruct((n * ROW_CHUNKS, V7X_LANES), F32),
                   jax.ShapeDtypeStruct((n, V7X_LANES), F32),
                   jax.ShapeDtypeStruct((V7X_SUBLANES, V7X_LANES), F32)],
        scratch_shapes=[pltpu.VMEM((1, V7X_LANES), F32)],
        compiler_params=_params(), name="xattn_router",
    )(h, g, wq, kt, v, wo, g_ffn, w_router, tri)


def _swiglu(hn, wg_ref, wu_ref, wd_ref):
    acc = None
    for start, size in FF_CHUNKS:
        gate = _dot(hn, wg_ref[:, start:start + size])
        up = _dot(hn, wu_ref[:, start:start + size])
        act = (gate * _sigmoid(gate) * up).astype(BF16)
        part = _dot(act, wd_ref[start:start + size, :])
        acc = part if acc is None else acc + part
    return acc


def _ffn_kernel(h_ref, g_ref, wg_ref, wu_ref, wd_ref, o_ref):
    h = h_ref[...]
    hn = _rms(h, g_ref[...]).astype(BF16)
    o_ref[...] = h + _swiglu(hn, wg_ref, wu_ref, wd_ref)


def _ffn(h, g, wg, wu, wd):
    n, d = h.shape
    tm = TOKEN_TILE
    row = lambda i: (i, 0)
    return pl.pallas_call(
        _ffn_kernel, grid=(n // tm,),
        in_specs=[pl.BlockSpec((tm, d), row), _resident(g.shape), _resident(wg.shape),
                  _resident(wu.shape), _resident(wd.shape)],
        out_specs=pl.BlockSpec((tm, d), row),
        out_shape=jax.ShapeDtypeStruct((n, d), F32),
        compiler_params=_params(), name="ffn",
    )(h, g, wg, wu, wd)


def _gmlp_kernel(h_ref, g_ref, w_in_ref, lng_ref, lnb_ref, ws_ref, bias_ref, w_out_ref, o_ref):
    tm = h_ref.shape[0]
    nblk = tm // C_BLOCK
    h = h_ref[...]
    hn = _rms(h, g_ref[...]).astype(BF16)
    u = _gelu_tanh(_dot(hn, w_in_ref[:, 0:C_WIDTH]))
    v = _gelu_tanh(_dot(hn, w_in_ref[:, C_WIDTH:2 * C_WIDTH]))
    v = _layer_norm(v, lng_ref[...], lnb_ref[...]).astype(BF16)

    qi = lax.broadcasted_iota(I32, (C_BLOCK, C_BLOCK), 0) // CHUNK
    kj = lax.broadcasted_iota(I32, (C_BLOCK, C_BLOCK), 1) // CHUNK
    causal = kj <= qi
    mixed = []
    for grp in range(C_GROUPS):
        lo, hi = grp * C_GROUP_DIM, (grp + 1) * C_GROUP_DIM
        w = jnp.where(causal, ws_ref[grp], 0.0).astype(BF16)
        rhs = jnp.concatenate([v[b * C_BLOCK:(b + 1) * C_BLOCK, lo:hi] for b in range(nblk)], axis=1)
        mixed.append(_dot(w, rhs))
    rows = []
    for b in range(nblk):
        blk = jnp.concatenate([m[:, b * C_GROUP_DIM:(b + 1) * C_GROUP_DIM] for m in mixed], axis=1)
        rows.append(blk + bias_ref[...])
    s = jnp.concatenate(rows, axis=0)
    o_ref[...] = h + _dot((u * s).astype(BF16), w_out_ref[...])


def _gmlp(h, g, w_in, lng, lnb, ws, bias, w_out):
    n, d = h.shape
    tm = TOKEN_TILE
    row = lambda i: (i, 0)
    return pl.pallas_call(
        _gmlp_kernel, grid=(n // tm,),
        in_specs=[pl.BlockSpec((tm, d), row), _resident(g.shape), _resident(w_in.shape),
                  _resident(lng.shape), _resident(lnb.shape), _resident(ws.shape),
                  _resident(bias.shape), _resident(w_out.shape)],
        out_specs=pl.BlockSpec((tm, d), row),
        out_shape=jax.ShapeDtypeStruct((n, d), F32),
        compiler_params=_params(), name="gmlp",
    )(h, g, w_in, lng, lnb, ws, bias, w_out)


def _expert_kernel(n_tokens, tile_expert_ref, n_tiles_ref, nxt_ref, cur_ref, hn_hbm, wg_ref, wu_ref,
                   wd_ref, y_hbm, xbuf, ybuf, gsem, ssem):
    j = pl.program_id(0)
    n_tiles = n_tiles_ref[0]
    tm = TOKEN_TILE
    rows_per_slot = tm * ROW_CHUNKS

    def row_tile(ref, r):
        return ref.at[pl.ds(pl.multiple_of(r * ROW_CHUNKS, ROW_CHUNKS), ROW_CHUNKS), :]

    def gather(codes_ref, slot):
        for r in range(tm):
            tok = codes_ref[r // V7X_LANES, r % V7X_LANES] & (n_tokens - 1)
            pltpu.make_async_copy(row_tile(hn_hbm, tok), row_tile(xbuf, slot * tm + r),
                                  gsem.at[slot]).start()

    def wait_gather(slot):
        pltpu.make_async_copy(hn_hbm.at[pl.ds(0, rows_per_slot), :],
                              xbuf.at[pl.ds(slot * rows_per_slot, rows_per_slot), :],
                              gsem.at[slot]).wait()

    def wait_scatter():
        pltpu.make_async_copy(ybuf, y_hbm.at[pl.ds(0, rows_per_slot), :], ssem).wait()

    @pl.when(j == 0)
    def _():
        gather(cur_ref, 0)
        ybuf[...] = jnp.zeros(ybuf.shape, F32)
        spare = y_hbm.at[pl.ds(TOP_K * n_tokens * ROW_CHUNKS, rows_per_slot), :]
        pltpu.make_async_copy(ybuf, spare, ssem).start()
        pltpu.make_async_copy(ybuf, spare, ssem).wait()

    @pl.when(j < n_tiles)
    def _():
        slot = j % 2
        wait_gather(slot)
        x = _load_row_tiles(xbuf, slot * rows_per_slot, tm).astype(BF16)
        gather(nxt_ref, 1 - slot)
        y = _swiglu(x, wg_ref, wu_ref, wd_ref)

        @pl.when(j > 0)
        def _():
            wait_scatter()

        _store_row_tiles(ybuf, y)
        for r in range(tm):
            code = cur_ref[r // V7X_LANES, r % V7X_LANES]
            pltpu.make_async_copy(row_tile(ybuf, r), row_tile(y_hbm, code), ssem).start()

        @pl.when(j == n_tiles - 1)
        def _():
            wait_gather(1 - slot)
            wait_scatter()


def _experts(hn_tiles, codes, tile_expert, n_tiles, wg, wu, wd, n_tokens):
    tm = TOKEN_TILE
    max_tiles = tile_expert.shape[0]
    code_rows = tm // V7X_LANES
    expert = lambda j, te, nt: (te[j], 0, 0)
    grid_spec = pltpu.PrefetchScalarGridSpec(
        num_scalar_prefetch=2, grid=(max_tiles,),
        in_specs=[pl.BlockSpec((None, code_rows, V7X_LANES),
                               lambda j, te, nt: (jnp.minimum(j + 1, max_tiles - 1), 0, 0),
                               memory_space=pltpu.SMEM),
                  pl.BlockSpec((None, code_rows, V7X_LANES), lambda j, te, nt: (j, 0, 0),
                               memory_space=pltpu.SMEM),
                  pl.BlockSpec(memory_space=pl.ANY),
                  pl.BlockSpec((None, D_MODEL, D_FF), expert),
                  pl.BlockSpec((None, D_MODEL, D_FF), expert),
                  pl.BlockSpec((None, D_FF, D_MODEL), expert)],
        out_specs=pl.BlockSpec(memory_space=pl.ANY),
        scratch_shapes=[pltpu.VMEM((2 * tm * ROW_CHUNKS, V7X_LANES), F32),
                        pltpu.VMEM((tm * ROW_CHUNKS, V7X_LANES), F32),
                        pltpu.SemaphoreType.DMA((2,)), pltpu.SemaphoreType.DMA(())])
    return pl.pallas_call(
        functools.partial(_expert_kernel, n_tokens), grid_spec=grid_spec,
        out_shape=jax.ShapeDtypeStruct(((TOP_K * n_tokens + tm) * ROW_CHUNKS, V7X_LANES), F32),
        compiler_params=_params(), name="experts",
    )(tile_expert, n_tiles, codes, codes, hn_tiles, wg, wu, wd)


def _tile_plan(route, counts, n_tokens):
    tm = TOKEN_TILE
    max_tiles = TOP_K * n_tokens // tm + N_EXPERTS
    counts = counts.astype(I32)
    tiles_e = (counts + tm - 1) // tm
    tile_end = jnp.cumsum(tiles_e)
    row_start = (tile_end - tiles_e) * tm
    n_tiles = tile_end[-1:]
    tile_expert = jnp.minimum(
        jnp.searchsorted(tile_end, jnp.arange(max_tiles, dtype=I32), side="right"),
        N_EXPERTS - 1).astype(I32)

    e1 = route[:, ROUTE_E1].astype(I32)
    e2 = route[:, ROUTE_E2].astype(I32)
    pos1 = row_start[e1] + route[:, ROUTE_R1].astype(I32)
    pos2 = row_start[e2] + route[:, ROUTE_R2].astype(I32)
    tok = jnp.arange(n_tokens, dtype=I32)
    pad_j = jnp.arange(tm, dtype=I32)[None, :]
    pad_used = pad_j < (tiles_e * tm - counts)[:, None]
    pad_pos = jnp.where(pad_used, (row_start + counts)[:, None] + pad_j,
                        max_tiles * tm + jnp.arange(N_EXPERTS, dtype=I32)[:, None] * tm + pad_j)
    pad_code = jnp.broadcast_to(TOP_K * n_tokens + pad_j, (N_EXPERTS, tm))
    keys = jnp.concatenate([pos1, pos2, pad_pos.reshape(-1)])
    vals = jnp.concatenate([tok, n_tokens + tok, pad_code.reshape(-1)])
    _, codes = lax.sort_key_val(keys, vals)
    return codes.reshape(max_tiles, tm // V7X_LANES, V7X_LANES), tile_expert, n_tiles


def _combine_kernel(h_ref, route_ref, y1_ref, y2_ref, gf_ref, o_ref):
    tm = h_ref.shape[0]
    w1 = route_ref[:, ROUTE_W1:ROUTE_W1 + 1]
    w2 = route_ref[:, ROUTE_W2:ROUTE_W2 + 1]
    y = w1 * _load_row_tiles(y1_ref, 0, tm) + w2 * _load_row_tiles(y2_ref, 0, tm)
    o_ref[...] = _rms(h_ref[...] + y, gf_ref[...])


def _combine(h, route, y_tiles, g_final):
    n, d = h.shape
    tm = TOKEN_TILE
    row = lambda i: (i, 0)
    second = n // tm
    return pl.pallas_call(
        _combine_kernel, grid=(n // tm,),
        in_specs=[pl.BlockSpec((tm, d), row), pl.BlockSpec((tm, V7X_LANES), row),
                  pl.BlockSpec((tm * ROW_CHUNKS, V7X_LANES), row),
                  pl.BlockSpec((tm * ROW_CHUNKS, V7X_LANES), lambda i: (i + second, 0)),
                  _resident(g_final.shape)],
        out_specs=pl.BlockSpec((tm, d), row),
        out_shape=jax.ShapeDtypeStruct((n, d), F32),
        compiler_params=_params(), name="combine",
    )(h, route, y_tiles, y_tiles, g_final)


def kernel(x, mem, norm_mix_g, norm_xattn_g, norm_mem_g, norm_ffn_g, final_norm_g, xa_w_q, xa_w_k, xa_w_v, xa_w_o, cv_w_in, cv_a_conv_w, cv_a_conv_b, cv_a_ln_g, cv_a_ln_b, cv_b_conv_w, cv_w_out, ffn_w_gate, ffn_w_up, ffn_w_down, sg_w_in, sg_ln_g, sg_ln_b, sg_w_s, sg_b_s, sg_w_out, moe_w_router, moe_w_gate, moe_w_up, moe_w_down):
    batch, seq, d = x.shape
    mem_len = mem.shape[1]
    n = batch * seq
    assert n & (n - 1) == 0 and seq % TOKEN_TILE == 0
    vec = lambda a: a.reshape(1, -1)
    bf = lambda a: a.astype(BF16)

    h = x.reshape(n, d)
    mem2d = mem.reshape(batch * mem_len, d)

    h = _conv_mixer(h, vec(norm_mix_g[0]), bf(cv_w_in[0]), cv_a_conv_w[0], vec(cv_a_conv_b[0]),
                    vec(cv_a_ln_g[0]), vec(cv_a_ln_b[0]), cv_b_conv_w[0], bf(cv_w_out[0]), seq)
    kt, v = _mem_kv(mem2d, vec(norm_mem_g[0]), bf(xa_w_k[0]), bf(xa_w_v[0]), batch, mem_len)
    h = _xattn(h, vec(norm_xattn_g[0]), bf(xa_w_q[0]), kt, v, bf(xa_w_o[0]), seq, mem_len)
    h = _ffn(h, vec(norm_ffn_g[0]), bf(ffn_w_gate[0]), bf(ffn_w_up[0]), bf(ffn_w_down[0]))

    bias = jnp.repeat(sg_b_s[0].T, C_GROUP_DIM, axis=1)
    h = _gmlp(h, vec(norm_mix_g[1]), bf(sg_w_in[0]), vec(sg_ln_g[0]), vec(sg_ln_b[0]), sg_w_s[0],
              bias, bf(sg_w_out[0]))
    kt, v = _mem_kv(mem2d, vec(norm_mem_g[1]), bf(xa_w_k[1]), bf(xa_w_v[1]), batch, mem_len)
    w_r = moe_w_router[0]
    w_r_hi = bf(w_r)
    w_r_lo = bf(w_r - w_r_hi.astype(F32))
    w_router = jnp.pad(jnp.concatenate([w_r_hi, w_r_lo], axis=1),
                       ((0, 0), (0, V7X_LANES - 2 * N_EXPERTS)))
    tri = jnp.tri(TOKEN_TILE, k=-1, dtype=BF16)
    h, hn_tiles, route, counts = _xattn(
        h, vec(norm_xattn_g[1]), bf(xa_w_q[1]), kt, v, bf(xa_w_o[1]), seq, mem_len,
        router=(vec(norm_ffn_g[1]), w_router, tri))
    codes, tile_expert, n_tiles = _tile_plan(route, counts[0, :N_EXPERTS], n)
    y_tiles = _experts(hn_tiles, codes, tile_expert, n_tiles, bf(moe_w_gate[0]), bf(moe_w_up[0]),
                       bf(moe_w_down[0]), n)
    out = _combine(h, route, y_tiles, vec(final_norm_g))
    return out.reshape(batch, seq, d)
```

```python
import functools

import jax
import jax.numpy as jnp
from jax import lax
from jax.experimental import pallas as pl
from jax.experimental.pallas import tpu as pltpu

F32 = jnp.float32
BF16 = jnp.bfloat16
I32 = jnp.int32

D_MODEL = 1024
A_WIDTH = 512
A_TAPS = 31
B_WIDTH = 512
B_TAPS = 3
C_WIDTH = 1024
C_GROUPS = 8
C_GROUP_DIM = 128
C_BLOCK = 128
CHUNK = 64
XA_HEADS = 4
XA_HEAD_DIM = 256
D_FF = 2816
N_EXPERTS = 8
TOP_K = 2
RMS_EPS = 1e-6
LN_EPS = 1e-5

V7X_LANES = 128
V7X_SUBLANES = 8
V7X_VMEM_LIMIT_BYTES = 60000 * 1024

TOKEN_TILE = 512
CONV_HALO = 32
CONV_ROW_CHUNK = 32
FF_CHUNKS = ((0, 512), (512, 512), (1024, 512), (1536, 512), (2048, 512), (2560, 256))
ROW_CHUNKS = D_MODEL // V7X_LANES
assert ROW_CHUNKS == V7X_SUBLANES
ROUTE_W1, ROUTE_W2, ROUTE_E1, ROUTE_E2, ROUTE_R1, ROUTE_R2 = range(6)


def _params(n_axes=1):
    return pltpu.CompilerParams(dimension_semantics=("arbitrary",) * n_axes,
                                vmem_limit_bytes=V7X_VMEM_LIMIT_BYTES)


def _resident(shape):
    return pl.BlockSpec(shape, lambda *_: (0,) * len(shape), pipeline_mode=pl.Buffered(1))


def _rms(x, g):
    return x * lax.rsqrt(jnp.mean(x * x, axis=-1, keepdims=True) + RMS_EPS) * g


def _layer_norm(x, g, b):
    mu = jnp.mean(x, axis=-1, keepdims=True)
    xc = x - mu
    var = jnp.mean(xc * xc, axis=-1, keepdims=True)
    return xc * lax.rsqrt(var + LN_EPS) * g + b


def _sigmoid(x):
    return 1.0 / (1.0 + jnp.exp(-x))


def _gelu_tanh(x):
    return 0.5 * x * (1.0 + jnp.tanh(0.7978845608028654 * (x + 0.044715 * (x * x * x))))


def _dot(a, b):
    return jnp.dot(a, b, preferred_element_type=F32)


def _store_row_tiles(ref, x):
    rows = x.shape[0]
    for s in range(ROW_CHUNKS):
        ref[pl.ds(s, rows, stride=ROW_CHUNKS), :] = x[:, s * V7X_LANES:(s + 1) * V7X_LANES]


def _load_row_tiles(ref, base, rows):
    return jnp.concatenate(
        [ref[pl.ds(base + s, rows, stride=ROW_CHUNKS), :] for s in range(ROW_CHUNKS)], axis=1)


def _conv_mixer_kernel(tiles_per_batch, x_ref, g_ref, w_in_ref, aw_ref, ab_ref, lng_ref, lnb_ref,
                       bw_ref, w_out_ref, o_ref, ext_ref, bext_ref, apost_ref):
    tm = x_ref.shape[0]
    a_tiles = A_WIDTH // V7X_LANES
    b_tiles = B_WIDTH // V7X_LANES
    lanes = lambda t: slice(t * V7X_LANES, (t + 1) * V7X_LANES)

    @pl.when(pl.program_id(0) % tiles_per_batch == 0)
    def _():
        ext_ref[:, 0:CONV_HALO, :] = jnp.zeros((a_tiles, CONV_HALO, V7X_LANES), F32)
        bext_ref[:, 0:V7X_SUBLANES, :] = jnp.zeros((b_tiles, V7X_SUBLANES, V7X_LANES), F32)

    x = x_ref[...]
    hn = _rms(x, g_ref[...]).astype(BF16)

    a_val = _dot(hn, w_in_ref[:, 0:A_WIDTH])
    a_gate = _dot(hn, w_in_ref[:, A_WIDTH:2 * A_WIDTH])
    a = a_val * _sigmoid(a_gate)
    for t in range(a_tiles):
        ext_ref[t, CONV_HALO:CONV_HALO + tm, :] = a[:, lanes(t)]

    base = 2 * A_WIDTH
    g_b = _dot(hn, w_in_ref[:, base:base + B_WIDTH])
    g_c = _dot(hn, w_in_ref[:, base + B_WIDTH:base + 2 * B_WIDTH])
    h_b = _dot(hn, w_in_ref[:, base + 2 * B_WIDTH:base + 3 * B_WIDTH])
    u = g_c * h_b
    conv_b = []
    for t in range(b_tiles):
        bext_ref[t, V7X_SUBLANES:V7X_SUBLANES + tm, :] = u[:, lanes(t)]
        acc = jnp.zeros((tm, V7X_LANES), F32)
        for k in range(B_TAPS):
            off = V7X_SUBLANES - (B_TAPS - 1) + k
            acc = acc + bw_ref[k:k + 1, lanes(t)] * bext_ref[t, off:off + tm, :]
        conv_b.append(acc)
        bext_ref[t, 0:V7X_SUBLANES, :] = bext_ref[t, tm:tm + V7X_SUBLANES, :]
    b_out = (g_b * jnp.concatenate(conv_b, axis=1)).astype(BF16)

    first = CONV_HALO - (A_TAPS - 1)
    for r0 in range(0, tm, CONV_ROW_CHUNK):
        parts = []
        for t in range(a_tiles):
            acc = jnp.zeros((CONV_ROW_CHUNK, V7X_LANES), F32) + ab_ref[:, lanes(t)]
            for k in range(A_TAPS):
                start = r0 + first + k
                acc = acc + aw_ref[k:k + 1, lanes(t)] * ext_ref[t, start:start + CONV_ROW_CHUNK, :]
            parts.append(acc)
        y = _layer_norm(jnp.concatenate(parts, axis=1), lng_ref[...], lnb_ref[...])
        apost_ref[r0:r0 + CONV_ROW_CHUNK, :] = (y * _sigmoid(y)).astype(BF16)
    for t in range(a_tiles):
        ext_ref[t, 0:CONV_HALO, :] = ext_ref[t, tm:tm + CONV_HALO, :]

    o_ref[...] = (x + _dot(apost_ref[...], w_out_ref[0:A_WIDTH, :])
                  + _dot(b_out, w_out_ref[A_WIDTH:A_WIDTH + B_WIDTH, :]))


def _conv_mixer(x, g, w_in, aw, ab, lng, lnb, bw, w_out, seq):
    n, d = x.shape
    tm = TOKEN_TILE
    row = lambda i: (i, 0)
    return pl.pallas_call(
        functools.partial(_conv_mixer_kernel, seq // tm),
        grid=(n // tm,),
        in_specs=[pl.BlockSpec((tm, d), row), _resident(g.shape), _resident(w_in.shape),
                  _resident(aw.shape), _resident(ab.shape), _resident(lng.shape), _resident(lnb.shape),
                  _resident(bw.shape), _resident(w_out.shape)],
        out_specs=pl.BlockSpec((tm, d), row),
        out_shape=jax.ShapeDtypeStruct((n, d), F32),
        scratch_shapes=[pltpu.VMEM((A_WIDTH // V7X_LANES, tm + CONV_HALO, V7X_LANES), F32),
                        pltpu.VMEM((B_WIDTH // V7X_LANES, tm + V7X_SUBLANES, V7X_LANES), F32),
                        pltpu.VMEM((tm, A_WIDTH), BF16)],
        compiler_params=_params(),
        name="conv_mixer",
    )(x, g, w_in, aw, ab, lng, lnb, bw, w_out)


def _mem_kv_kernel(mem_ref, g_ref, wk_ref, wv_ref, kt_ref, v_ref):
    mn = _rms(mem_ref[...], g_ref[...]).astype(BF16)
    kt_ref[...] = _dot(mn, wk_ref[...]).T.astype(BF16)
    v_ref[...] = _dot(mn, wv_ref[...]).astype(BF16)


def _mem_kv(mem2d, g, wk, wv, batch, mem_len):
    d = mem2d.shape[1]
    return pl.pallas_call(
        _mem_kv_kernel,
        grid=(batch,),
        in_specs=[pl.BlockSpec((mem_len, d), lambda b: (b, 0)), _resident(g.shape),
                  _resident(wk.shape), _resident(wv.shape)],
        out_specs=[pl.BlockSpec((d, mem_len), lambda b: (b, 0)),
                   pl.BlockSpec((mem_len, d), lambda b: (b, 0))],
        out_shape=[jax.ShapeDtypeStruct((batch * d, mem_len), BF16),
                   jax.ShapeDtypeStruct((batch * mem_len, d), BF16)],
        compiler_params=_params(),
        name="mem_kv",
    )(mem2d, g, wk, wv)


def _attend(h, g_ref, wq_ref, kt_ref, v_ref, wo_ref):
    hn = _rms(h, g_ref[...]).astype(BF16)
    q = (_dot(hn, wq_ref[...]) * (XA_HEAD_DIM ** -0.5)).astype(BF16)
    heads = []
    for hd in range(XA_HEADS):
        lo, hi = hd * XA_HEAD_DIM, (hd + 1) * XA_HEAD_DIM
        s = _dot(q[:, lo:hi], kt_ref[lo:hi, :])
        p = jnp.exp(s - jnp.max(s, axis=-1, keepdims=True))
        p = p * (1.0 / jnp.sum(p, axis=-1, keepdims=True))
        heads.append(_dot(p.astype(BF16), v_ref[:, lo:hi]))
    o = jnp.concatenate(heads, axis=-1).astype(BF16)
    return h + _dot(o, wo_ref[...])


def _xattn_kernel(h_ref, g_ref, wq_ref, kt_ref, v_ref, wo_ref, o_ref):
    o_ref[...] = _attend(h_ref[...], g_ref, wq_ref, kt_ref, v_ref, wo_ref)


def _xattn_router_kernel(h_ref, g_ref, wq_ref, kt_ref, v_ref, wo_ref, gf_ref, wr_ref, tri_ref,
                         o_ref, hn_ref, route_ref, route_t_ref, counts_ref, seen_ref):
    @pl.when(pl.program_id(0) == 0)
    def _():
        seen_ref[...] = jnp.zeros(seen_ref.shape, F32)

    h = _attend(h_ref[...], g_ref, wq_ref, kt_ref, v_ref, wo_ref)
    o_ref[...] = h
    hn = _rms(h, gf_ref[...])
    _store_row_tiles(hn_ref, hn)

    hn_hi = hn.astype(BF16)
    hn_lo = (hn - hn_hi.astype(F32)).astype(BF16)
    p_hi = _dot(hn_hi, wr_ref[...])
    p_lo = _dot(hn_lo, wr_ref[...])
    logits = (pltpu.roll(p_hi, V7X_LANES - N_EXPERTS, axis=1) + p_lo) + p_hi

    lane = lax.broadcasted_iota(I32, logits.shape, 1)
    neg = jnp.float32(-jnp.inf)
    logits = jnp.where(lane < N_EXPERTS, logits, neg)
    m1 = jnp.max(logits, axis=-1, keepdims=True)
    e1 = jnp.min(jnp.where(logits == m1, lane, V7X_LANES), axis=-1, keepdims=True)
    rest = jnp.where(lane == e1, neg, logits)
    m2 = jnp.max(rest, axis=-1, keepdims=True)
    e2 = jnp.min(jnp.where(rest == m2, lane, V7X_LANES), axis=-1, keepdims=True)
    ex = jnp.exp(m2 - m1)
    w1 = 1.0 / (1.0 + ex)
    w2 = ex * w1

    chosen = (lane == e1) | (lane == e2)
    member = jnp.where(chosen, 1.0, 0.0)
    rank = _dot(tri_ref[...], member.astype(BF16)) + seen_ref[...]
    seen_ref[...] = seen_ref[...] + jnp.sum(member, axis=0, keepdims=True)
    counts_ref[...] = jnp.broadcast_to(seen_ref[...], counts_ref.shape)
    r1 = jnp.sum(jnp.where(lane == e1, rank, 0.0), axis=-1, keepdims=True)
    r2 = jnp.sum(jnp.where(lane == e2, rank, 0.0), axis=-1, keepdims=True)

    record = jnp.zeros(logits.shape, F32)
    for slot, val in ((ROUTE_W1, w1), (ROUTE_W2, w2), (ROUTE_E1, e1.astype(F32)),
                      (ROUTE_E2, e2.astype(F32)), (ROUTE_R1, r1), (ROUTE_R2, r2)):
        record = jnp.where(lane == slot, val, record)
    route_ref[...] = record
    route_t_ref[...] = record.T[0:V7X_SUBLANES, :]


def _xattn(h, g, wq, kt, v, wo, seq, mem_len, router=None):
    n, d = h.shape
    tm = TOKEN_TILE
    tpb = seq // tm
    row = lambda i: (i, 0)
    in_specs = [pl.BlockSpec((tm, d), row), _resident(g.shape), _resident(wq.shape),
                pl.BlockSpec((d, mem_len), lambda i: (i // tpb, 0)),
                pl.BlockSpec((mem_len, d), lambda i: (i // tpb, 0)),
                _resident(wo.shape)]
    if router is None:
        return pl.pallas_call(
            _xattn_kernel, grid=(n // tm,), in_specs=in_specs,
            out_specs=pl.BlockSpec((tm, d), row),
            out_shape=jax.ShapeDtypeStruct((n, d), F32),
            compiler_params=_params(), name="xattn",
        )(h, g, wq, kt, v, wo)
    g_ffn, w_router, tri = router
    return pl.pallas_call(
        _xattn_router_kernel, grid=(n // tm,),
        in_specs=in_specs + [_resident(g_ffn.shape), _resident(w_router.shape), _resident(tri.shape)],
        out_specs=[pl.BlockSpec((tm, d), row), pl.BlockSpec((tm * ROW_CHUNKS, V7X_LANES), row),
                   pl.BlockSpec((tm, V7X_LANES), row),
                   pl.BlockSpec((V7X_SUBLANES, tm), lambda i: (0, i)),
                   pl.BlockSpec((V7X_SUBLANES, V7X_LANES), lambda i: (0, 0))],
        out_shape=[jax.ShapeDtypeStruct((n, d), F32),
                   jax.ShapeDtypeStruct((n * ROW_CHUNKS, V7X_LANES), F32),
                   jax.ShapeDtypeStruct((n, V7X_LANES), F32),
                   jax.ShapeDtypeStruct((V7X_SUBLANES, n), F32),
                   jax.ShapeDtypeStruct((V7X_SUBLANES, V7X_LANES), F32)],
        scratch_shapes=[pltpu.VMEM((1, V7X_LANES), F32)],
        compiler_params=_params(), name="xattn_router",
    )(h, g, wq, kt, v, wo, g_ffn, w_router, tri)


def _swiglu(hn, wg_ref, wu_ref, wd_ref):
    acc = None
    for start, size in FF_CHUNKS:
        gate = _dot(hn, wg_ref[:, start:start + size])
        up = _dot(hn, wu_ref[:, start:start + size])
        act = (gate * _sigmoid(gate) * up).astype(BF16)
        part = _dot(act, wd_ref[start:start + size, :])
        acc = part if acc is None else acc + part
    return acc


def _ffn_kernel(h_ref, g_ref, wg_ref, wu_ref, wd_ref, o_ref):
    h = h_ref[...]
    hn = _rms(h, g_ref[...]).astype(BF16)
    o_ref[...] = h + _swiglu(hn, wg_ref, wu_ref, wd_ref)


def _ffn(h, g, wg, wu, wd):
    n, d = h.shape
    tm = TOKEN_TILE
    row = lambda i: (i, 0)
    return pl.pallas_call(
        _ffn_kernel, grid=(n // tm,),
        in_specs=[pl.BlockSpec((tm, d), row), _resident(g.shape), _resident(wg.shape),
                  _resident(wu.shape), _resident(wd.shape)],
        out_specs=pl.BlockSpec((tm, d), row),
        out_shape=jax.ShapeDtypeStruct((n, d), F32),
        compiler_params=_params(), name="ffn",
    )(h, g, wg, wu, wd)


def _gmlp_kernel(h_ref, g_ref, w_in_ref, lng_ref, lnb_ref, ws_ref, bias_ref, w_out_ref, o_ref):
    tm = h_ref.shape[0]
    nblk = tm // C_BLOCK
    h = h_ref[...]
    hn = _rms(h, g_ref[...]).astype(BF16)
    u = _gelu_tanh(_dot(hn, w_in_ref[:, 0:C_WIDTH]))
    v = _gelu_tanh(_dot(hn, w_in_ref[:, C_WIDTH:2 * C_WIDTH]))
    v = _layer_norm(v, lng_ref[...], lnb_ref[...]).astype(BF16)

    qi = lax.broadcasted_iota(I32, (C_BLOCK, C_BLOCK), 0) // CHUNK
    kj = lax.broadcasted_iota(I32, (C_BLOCK, C_BLOCK), 1) // CHUNK
    causal = kj <= qi
    mixed = []
    for grp in range(C_GROUPS):
        lo, hi = grp * C_GROUP_DIM, (grp + 1) * C_GROUP_DIM
        w = jnp.where(causal, ws_ref[grp], 0.0).astype(BF16)
        rhs = jnp.concatenate([v[b * C_BLOCK:(b + 1) * C_BLOCK, lo:hi] for b in range(nblk)], axis=1)
        mixed.append(_dot(w, rhs))
    rows = []
    for b in range(nblk):
        blk = jnp.concatenate([m[:, b * C_GROUP_DIM:(b + 1) * C_GROUP_DIM] for m in mixed], axis=1)
        rows.append(blk + bias_ref[...])
    s = jnp.concatenate(rows, axis=0)
    o_ref[...] = h + _dot((u * s).astype(BF16), w_out_ref[...])


def _gmlp(h, g, w_in, lng, lnb, ws, bias, w_out):
    n, d = h.shape
    tm = TOKEN_TILE
    row = lambda i: (i, 0)
    return pl.pallas_call(
        _gmlp_kernel, grid=(n // tm,),
        in_specs=[pl.BlockSpec((tm, d), row), _resident(g.shape), _resident(w_in.shape),
                  _resident(lng.shape), _resident(lnb.shape), _resident(ws.shape),
                  _resident(bias.shape), _resident(w_out.shape)],
        out_specs=pl.BlockSpec((tm, d), row),
        out_shape=jax.ShapeDtypeStruct((n, d), F32),
        compiler_params=_params(), name="gmlp",
    )(h, g, w_in, lng, lnb, ws, bias, w_out)


def _row_tile(ref, r):
    return ref.at[pl.ds(pl.multiple_of(r * ROW_CHUNKS, ROW_CHUNKS), ROW_CHUNKS), :]


def _gather_rows(src_hbm, rows_ref, n_rows, dst, sem):
    for i in range(n_rows):
        r = rows_ref[i // V7X_LANES, i % V7X_LANES]
        pltpu.make_async_copy(_row_tile(src_hbm, r),
                              dst.at[i * ROW_CHUNKS:(i + 1) * ROW_CHUNKS, :], sem).start()


def _wait_rows(src_hbm, dst, sem):
    pltpu.make_async_copy(src_hbm.at[pl.ds(0, dst.shape[0]), :], dst, sem).wait()


def _expert_kernel(tile_expert_ref, n_tiles_ref, nxt_ref, cur_ref, hn_hbm, wg_ref, wu_ref, wd_ref,
                   y_ref, xbuf0, xbuf1, gsem):
    j = pl.program_id(0)
    n_tiles = n_tiles_ref[0]
    tm = TOKEN_TILE
    xbufs = (xbuf0, xbuf1)

    @pl.when(j == 0)
    def _():
        _gather_rows(hn_hbm, cur_ref, tm, xbuf0, gsem.at[0])

    def tile(slot):
        _wait_rows(hn_hbm, xbufs[slot], gsem.at[slot])
        x = _load_row_tiles(xbufs[slot], 0, tm).astype(BF16)
        _gather_rows(hn_hbm, nxt_ref, tm, xbufs[1 - slot], gsem.at[1 - slot])
        _store_row_tiles(y_ref, _swiglu(x, wg_ref, wu_ref, wd_ref))

        @pl.when(j == n_tiles - 1)
        def _():
            _wait_rows(hn_hbm, xbufs[1 - slot], gsem.at[1 - slot])

    for slot in range(2):
        pl.when((j < n_tiles) & (j % 2 == slot))(functools.partial(tile, slot))

    @pl.when(j >= n_tiles)
    def _():
        y_ref[...] = jnp.zeros(y_ref.shape, F32)


def _experts(hn_tiles, tile_tokens, tile_expert, n_tiles, wg, wu, wd):
    tm = TOKEN_TILE
    max_tiles = tile_expert.shape[0]
    code_rows = tm // V7X_LANES
    expert = lambda j, te, nt: (te[j], 0, 0)
    grid_spec = pltpu.PrefetchScalarGridSpec(
        num_scalar_prefetch=2, grid=(max_tiles,),
        in_specs=[pl.BlockSpec((None, code_rows, V7X_LANES),
                               lambda j, te, nt: (jnp.minimum(j + 1, max_tiles - 1), 0, 0),
                               memory_space=pltpu.SMEM),
                  pl.BlockSpec((None, code_rows, V7X_LANES), lambda j, te, nt: (j, 0, 0),
                               memory_space=pltpu.SMEM),
                  pl.BlockSpec(memory_space=pl.ANY),
                  pl.BlockSpec((None, D_MODEL, D_FF), expert),
                  pl.BlockSpec((None, D_MODEL, D_FF), expert),
                  pl.BlockSpec((None, D_FF, D_MODEL), expert)],
        out_specs=pl.BlockSpec((tm * ROW_CHUNKS, V7X_LANES), lambda j, te, nt: (j, 0)),
        scratch_shapes=[pltpu.VMEM((tm * ROW_CHUNKS, V7X_LANES), F32),
                        pltpu.VMEM((tm * ROW_CHUNKS, V7X_LANES), F32),
                        pltpu.SemaphoreType.DMA((2,))])
    return pl.pallas_call(
        _expert_kernel, grid_spec=grid_spec,
        out_shape=jax.ShapeDtypeStruct((max_tiles * tm * ROW_CHUNKS, V7X_LANES), F32),
        compiler_params=_params(), name="experts",
    )(tile_expert, n_tiles, tile_tokens, tile_tokens, hn_tiles, wg, wu, wd)


def _tile_plan(route_t, counts, n_tokens):
    tm = TOKEN_TILE
    max_tiles = TOP_K * n_tokens // tm + N_EXPERTS
    counts = counts.astype(I32)
    tiles_e = (counts + tm - 1) // tm
    tile_end = jnp.cumsum(tiles_e)
    row_start = (tile_end - tiles_e) * tm
    n_tiles = tile_end[-1:]
    tile_ids = jnp.arange(max_tiles, dtype=I32)
    tile_expert = jnp.minimum(jnp.sum(tile_end[None, :] <= tile_ids[:, None], axis=1),
                              N_EXPERTS - 1).astype(I32)

    def row_of(e_row, r_row):
        e = route_t[e_row].astype(I32)
        start = jnp.zeros_like(e)
        for k in range(N_EXPERTS):
            start = jnp.where(e == k, row_start[k], start)
        return start + route_t[r_row].astype(I32)

    pos1 = row_of(ROUTE_E1, ROUTE_R1)
    pos2 = row_of(ROUTE_E2, ROUTE_R2)
    tok = jnp.arange(n_tokens, dtype=I32)
    pad_j = jnp.arange(tm, dtype=I32)[None, :]
    pad_used = pad_j < (tiles_e * tm - counts)[:, None]
    pad_pos = jnp.where(pad_used, (row_start + counts)[:, None] + pad_j,
                        max_tiles * tm + jnp.arange(N_EXPERTS, dtype=I32)[:, None] * tm + pad_j)
    pad_tok = jnp.broadcast_to(pad_j, (N_EXPERTS, tm))
    keys = jnp.concatenate([pos1, pos2, pad_pos.reshape(-1)])
    vals = jnp.concatenate([tok, tok, pad_tok.reshape(-1)])
    _, tile_tokens = lax.sort_key_val(keys, vals)
    code_rows = tm // V7X_LANES
    token_rows = jnp.concatenate([pos1.reshape(n_tokens // tm, code_rows, V7X_LANES),
                                  pos2.reshape(n_tokens // tm, code_rows, V7X_LANES)], axis=1)
    return tile_tokens.reshape(max_tiles, code_rows, V7X_LANES), tile_expert, n_tiles, token_rows


def _combine_kernel(nxt_ref, cur_ref, h_ref, route_ref, y_hbm, gf_ref, o_ref, ybuf0, ybuf1, gsem):
    i = pl.program_id(0)
    tm = h_ref.shape[0]
    ybufs = (ybuf0, ybuf1)

    @pl.when(i == 0)
    def _():
        _gather_rows(y_hbm, cur_ref, TOP_K * tm, ybuf0, gsem.at[0])

    def tile(slot):
        _wait_rows(y_hbm, ybufs[slot], gsem.at[slot])

        @pl.when(i + 1 < pl.num_programs(0))
        def _():
            _gather_rows(y_hbm, nxt_ref, TOP_K * tm, ybufs[1 - slot], gsem.at[1 - slot])

        w1 = route_ref[:, ROUTE_W1:ROUTE_W1 + 1]
        w2 = route_ref[:, ROUTE_W2:ROUTE_W2 + 1]
        y = (w1 * _load_row_tiles(ybufs[slot], 0, tm)
             + w2 * _load_row_tiles(ybufs[slot], tm * ROW_CHUNKS, tm))
        o_ref[...] = _rms(h_ref[...] + y, gf_ref[...])

    for slot in range(2):
        pl.when(i % 2 == slot)(functools.partial(tile, slot))


def _combine(h, route, y_tiles, token_rows, g_final):
    n, d = h.shape
    tm = TOKEN_TILE
    steps = n // tm
    row = lambda i: (i, 0)
    rows_block = (None,) + token_rows.shape[1:]
    return pl.pallas_call(
        _combine_kernel, grid=(steps,),
        in_specs=[pl.BlockSpec(rows_block, lambda i: (jnp.minimum(i + 1, steps - 1), 0, 0),
                               memory_space=pltpu.SMEM),
                  pl.BlockSpec(rows_block, lambda i: (i, 0, 0), memory_space=pltpu.SMEM),
                  pl.BlockSpec((tm, d), row), pl.BlockSpec((tm, V7X_LANES), row),
                  pl.BlockSpec(memory_space=pl.ANY), _resident(g_final.shape)],
        out_specs=pl.BlockSpec((tm, d), row),
        out_shape=jax.ShapeDtypeStruct((n, d), F32),
        scratch_shapes=[pltpu.VMEM((TOP_K * tm * ROW_CHUNKS, V7X_LANES), F32),
                        pltpu.VMEM((TOP_K * tm * ROW_CHUNKS, V7X_LANES), F32),
                        pltpu.SemaphoreType.DMA((2,))],
        compiler_params=_params(), name="combine",
    )(token_rows, token_rows, h, route, y_tiles, g_final)


def kernel(x, mem, norm_mix_g, norm_xattn_g, norm_mem_g, norm_ffn_g, final_norm_g, xa_w_q, xa_w_k, xa_w_v, xa_w_o, cv_w_in, cv_a_conv_w, cv_a_conv_b, cv_a_ln_g, cv_a_ln_b, cv_b_conv_w, cv_w_out, ffn_w_gate, ffn_w_up, ffn_w_down, sg_w_in, sg_ln_g, sg_ln_b, sg_w_s, sg_b_s, sg_w_out, moe_w_router, moe_w_gate, moe_w_up, moe_w_down):
    batch, seq, d = x.shape
    mem_len = mem.shape[1]
    n = batch * seq
    assert n & (n - 1) == 0 and seq % TOKEN_TILE == 0
    vec = lambda a: a.reshape(1, -1)
    bf = lambda a: a.astype(BF16)

    h = x.reshape(n, d)
    mem2d = mem.reshape(batch * mem_len, d)

    h = _conv_mixer(h, vec(norm_mix_g[0]), bf(cv_w_in[0]), cv_a_conv_w[0], vec(cv_a_conv_b[0]),
                    vec(cv_a_ln_g[0]), vec(cv_a_ln_b[0]), cv_b_conv_w[0], bf(cv_w_out[0]), seq)
    kt, v = _mem_kv(mem2d, vec(norm_mem_g[0]), bf(xa_w_k[0]), bf(xa_w_v[0]), batch, mem_len)
    h = _xattn(h, vec(norm_xattn_g[0]), bf(xa_w_q[0]), kt, v, bf(xa_w_o[0]), seq, mem_len)
    h = _ffn(h, vec(norm_ffn_g[0]), bf(ffn_w_gate[0]), bf(ffn_w_up[0]), bf(ffn_w_down[0]))

    bias = jnp.repeat(sg_b_s[0].T, C_GROUP_DIM, axis=1)
    h = _gmlp(h, vec(norm_mix_g[1]), bf(sg_w_in[0]), vec(sg_ln_g[0]), vec(sg_ln_b[0]), sg_w_s[0],
              bias, bf(sg_w_out[0]))
    kt, v = _mem_kv(mem2d, vec(norm_mem_g[1]), bf(xa_w_k[1]), bf(xa_w_v[1]), batch, mem_len)
    w_r = moe_w_router[0]
    w_r_hi = bf(w_r)
    w_r_lo = bf(w_r - w_r_hi.astype(F32))
    w_router = jnp.pad(jnp.concatenate([w_r_hi, w_r_lo], axis=1),
                       ((0, 0), (0, V7X_LANES - 2 * N_EXPERTS)))
    tri = jnp.tri(TOKEN_TILE, k=-1, dtype=BF16)
    h, hn_tiles, route, route_t, counts = _xattn(
        h, vec(norm_xattn_g[1]), bf(xa_w_q[1]), kt, v, bf(xa_w_o[1]), seq, mem_len,
        router=(vec(norm_ffn_g[1]), w_router, tri))
    tile_tokens, tile_expert, n_tiles, token_rows = _tile_plan(route_t, counts[0, :N_EXPERTS], n)
    y_tiles = _experts(hn_tiles, tile_tokens, tile_expert, n_tiles, bf(moe_w_gate[0]),
                       bf(moe_w_up[0]), bf(moe_w_down[0]))
    out = _combine(h, route, y_tiles, token_rows, vec(final_norm_g))
    return out.reshape(batch, seq, d)
```

```python
import functools

import jax
import jax.numpy as jnp
from jax import lax
from jax.experimental import pallas as pl
from jax.experimental.pallas import tpu as pltpu

F32 = jnp.float32
BF16 = jnp.bfloat16
I32 = jnp.int32

D_MODEL = 1024
A_WIDTH = 512
A_TAPS = 31
B_WIDTH = 512
B_TAPS = 3
C_WIDTH = 1024
C_GROUPS = 8
C_GROUP_DIM = 128
C_BLOCK = 128
CHUNK = 64
XA_HEADS = 4
XA_HEAD_DIM = 256
D_FF = 2816
N_EXPERTS = 8
TOP_K = 2
RMS_EPS = 1e-6
LN_EPS = 1e-5

V7X_LANES = 128
V7X_SUBLANES = 8
V7X_VMEM_LIMIT_BYTES = 60000 * 1024
DMA_PRIORITIES = 2

TOKEN_TILE = 512
CONV_HALO = 32
CONV_ROW_CHUNK = 32
FF_CHUNKS = ((0, 512), (512, 512), (1024, 512), (1536, 512), (2048, 512), (2560, 256))
ROW_CHUNKS = D_MODEL // V7X_LANES
assert ROW_CHUNKS == V7X_SUBLANES
ROUTE_W1, ROUTE_W2, ROUTE_E1, ROUTE_E2, ROUTE_R1, ROUTE_R2 = range(6)


def _params(n_axes=1):
    return pltpu.CompilerParams(dimension_semantics=("arbitrary",) * n_axes,
                                vmem_limit_bytes=V7X_VMEM_LIMIT_BYTES)


def _resident(shape):
    return pl.BlockSpec(shape, lambda *_: (0,) * len(shape), pipeline_mode=pl.Buffered(1))


def _rms(x, g):
    return x * lax.rsqrt(jnp.mean(x * x, axis=-1, keepdims=True) + RMS_EPS) * g


def _layer_norm(x, g, b):
    mu = jnp.mean(x, axis=-1, keepdims=True)
    xc = x - mu
    var = jnp.mean(xc * xc, axis=-1, keepdims=True)
    return xc * lax.rsqrt(var + LN_EPS) * g + b


def _sigmoid(x):
    return 1.0 / (1.0 + jnp.exp(-x))


def _gelu_tanh(x):
    return 0.5 * x * (1.0 + jnp.tanh(0.7978845608028654 * (x + 0.044715 * (x * x * x))))


def _dot(a, b):
    return jnp.dot(a, b, preferred_element_type=F32)


def _store_row_tiles(ref, x):
    rows = x.shape[0]
    for s in range(ROW_CHUNKS):
        ref[pl.ds(s, rows, stride=ROW_CHUNKS), :] = x[:, s * V7X_LANES:(s + 1) * V7X_LANES]


def _load_row_tiles(ref, base, rows):
    return jnp.concatenate(
        [ref[pl.ds(base + s, rows, stride=ROW_CHUNKS), :] for s in range(ROW_CHUNKS)], axis=1)


def _conv_mixer_kernel(tiles_per_batch, x_ref, g_ref, w_in_ref, aw_ref, ab_ref, lng_ref, lnb_ref,
                       bw_ref, w_out_ref, o_ref, ext_ref, bext_ref, apost_ref):
    tm = x_ref.shape[0]
    a_tiles = A_WIDTH // V7X_LANES
    b_tiles = B_WIDTH // V7X_LANES
    lanes = lambda t: slice(t * V7X_LANES, (t + 1) * V7X_LANES)

    @pl.when(pl.program_id(0) % tiles_per_batch == 0)
    def _():
        ext_ref[:, 0:CONV_HALO, :] = jnp.zeros((a_tiles, CONV_HALO, V7X_LANES), F32)
        bext_ref[:, 0:V7X_SUBLANES, :] = jnp.zeros((b_tiles, V7X_SUBLANES, V7X_LANES), F32)

    x = x_ref[...]
    hn = _rms(x, g_ref[...]).astype(BF16)

    a_val = _dot(hn, w_in_ref[:, 0:A_WIDTH])
    a_gate = _dot(hn, w_in_ref[:, A_WIDTH:2 * A_WIDTH])
    a = a_val * _sigmoid(a_gate)
    for t in range(a_tiles):
        ext_ref[t, CONV_HALO:CONV_HALO + tm, :] = a[:, lanes(t)]

    base = 2 * A_WIDTH
    g_b = _dot(hn, w_in_ref[:, base:base + B_WIDTH])
    g_c = _dot(hn, w_in_ref[:, base + B_WIDTH:base + 2 * B_WIDTH])
    h_b = _dot(hn, w_in_ref[:, base + 2 * B_WIDTH:base + 3 * B_WIDTH])
    u = g_c * h_b
    conv_b = []
    for t in range(b_tiles):
        bext_ref[t, V7X_SUBLANES:V7X_SUBLANES + tm, :] = u[:, lanes(t)]
        acc = jnp.zeros((tm, V7X_LANES), F32)
        for k in range(B_TAPS):
            off = V7X_SUBLANES - (B_TAPS - 1) + k
            acc = acc + bw_ref[k:k + 1, lanes(t)] * bext_ref[t, off:off + tm, :]
        conv_b.append(acc)
        bext_ref[t, 0:V7X_SUBLANES, :] = bext_ref[t, tm:tm + V7X_SUBLANES, :]
    b_out = (g_b * jnp.concatenate(conv_b, axis=1)).astype(BF16)

    first = CONV_HALO - (A_TAPS - 1)
    for r0 in range(0, tm, CONV_ROW_CHUNK):
        parts = []
        for t in range(a_tiles):
            acc = jnp.zeros((CONV_ROW_CHUNK, V7X_LANES), F32) + ab_ref[:, lanes(t)]
            for k in range(A_TAPS):
                start = r0 + first + k
                acc = acc + aw_ref[k:k + 1, lanes(t)] * ext_ref[t, start:start + CONV_ROW_CHUNK, :]
            parts.append(acc)
        y = _layer_norm(jnp.concatenate(parts, axis=1), lng_ref[...], lnb_ref[...])
        apost_ref[r0:r0 + CONV_ROW_CHUNK, :] = (y * _sigmoid(y)).astype(BF16)
    for t in range(a_tiles):
        ext_ref[t, 0:CONV_HALO, :] = ext_ref[t, tm:tm + CONV_HALO, :]

    o_ref[...] = (x + _dot(apost_ref[...], w_out_ref[0:A_WIDTH, :])
                  + _dot(b_out, w_out_ref[A_WIDTH:A_WIDTH + B_WIDTH, :]))


def _conv_mixer(x, g, w_in, aw, ab, lng, lnb, bw, w_out, seq):
    n, d = x.shape
    tm = TOKEN_TILE
    row = lambda i: (i, 0)
    return pl.pallas_call(
        functools.partial(_conv_mixer_kernel, seq // tm),
        grid=(n // tm,),
        in_specs=[pl.BlockSpec((tm, d), row), _resident(g.shape), _resident(w_in.shape),
                  _resident(aw.shape), _resident(ab.shape), _resident(lng.shape), _resident(lnb.shape),
                  _resident(bw.shape), _resident(w_out.shape)],
        out_specs=pl.BlockSpec((tm, d), row),
        out_shape=jax.ShapeDtypeStruct((n, d), F32),
        scratch_shapes=[pltpu.VMEM((A_WIDTH // V7X_LANES, tm + CONV_HALO, V7X_LANES), F32),
                        pltpu.VMEM((B_WIDTH // V7X_LANES, tm + V7X_SUBLANES, V7X_LANES), F32),
                        pltpu.VMEM((tm, A_WIDTH), BF16)],
        compiler_params=_params(),
        name="conv_mixer",
    )(x, g, w_in, aw, ab, lng, lnb, bw, w_out)


def _mem_kv_kernel(mem_ref, g_ref, wk_ref, wv_ref, kt_ref, v_ref):
    mn = _rms(mem_ref[...], g_ref[...]).astype(BF16)
    kt_ref[...] = _dot(mn, wk_ref[...]).T.astype(BF16)
    v_ref[...] = _dot(mn, wv_ref[...]).astype(BF16)


def _mem_kv(mem2d, g, wk, wv, batch, mem_len):
    d = mem2d.shape[1]
    return pl.pallas_call(
        _mem_kv_kernel,
        grid=(batch,),
        in_specs=[pl.BlockSpec((mem_len, d), lambda b: (b, 0)), _resident(g.shape),
                  _resident(wk.shape), _resident(wv.shape)],
        out_specs=[pl.BlockSpec((d, mem_len), lambda b: (b, 0)),
                   pl.BlockSpec((mem_len, d), lambda b: (b, 0))],
        out_shape=[jax.ShapeDtypeStruct((batch * d, mem_len), BF16),
                   jax.ShapeDtypeStruct((batch * mem_len, d), BF16)],
        compiler_params=_params(),
        name="mem_kv",
    )(mem2d, g, wk, wv)


def _attend(h, g_ref, wq_ref, kt_ref, v_ref, wo_ref):
    hn = _rms(h, g_ref[...]).astype(BF16)
    q = (_dot(hn, wq_ref[...]) * (XA_HEAD_DIM ** -0.5)).astype(BF16)
    heads = []
    for hd in range(XA_HEADS):
        lo, hi = hd * XA_HEAD_DIM, (hd + 1) * XA_HEAD_DIM
        s = _dot(q[:, lo:hi], kt_ref[lo:hi, :])
        p = jnp.exp(s - jnp.max(s, axis=-1, keepdims=True))
        p = p * (1.0 / jnp.sum(p, axis=-1, keepdims=True))
        heads.append(_dot(p.astype(BF16), v_ref[:, lo:hi]))
    o = jnp.concatenate(heads, axis=-1).astype(BF16)
    return h + _dot(o, wo_ref[...])


def _xattn_kernel(h_ref, g_ref, wq_ref, kt_ref, v_ref, wo_ref, o_ref):
    o_ref[...] = _attend(h_ref[...], g_ref, wq_ref, kt_ref, v_ref, wo_ref)


def _xattn_router_kernel(h_ref, g_ref, wq_ref, kt_ref, v_ref, wo_ref, gf_ref, wr_ref, tri_ref,
                         o_ref, hn_ref, route_ref, route_t_ref, counts_ref, seen_ref):
    @pl.when(pl.program_id(0) == 0)
    def _():
        seen_ref[...] = jnp.zeros(seen_ref.shape, F32)

    h = _attend(h_ref[...], g_ref, wq_ref, kt_ref, v_ref, wo_ref)
    o_ref[...] = h
    hn = _rms(h, gf_ref[...])
    _store_row_tiles(hn_ref, hn)

    hn_hi = hn.astype(BF16)
    hn_lo = (hn - hn_hi.astype(F32)).astype(BF16)
    p_hi = _dot(hn_hi, wr_ref[...])
    p_lo = _dot(hn_lo, wr_ref[...])
    logits = (pltpu.roll(p_hi, V7X_LANES - N_EXPERTS, axis=1) + p_lo) + p_hi

    lane = lax.broadcasted_iota(I32, logits.shape, 1)
    neg = jnp.float32(-jnp.inf)
    logits = jnp.where(lane < N_EXPERTS, logits, neg)
    m1 = jnp.max(logits, axis=-1, keepdims=True)
    e1 = jnp.min(jnp.where(logits == m1, lane, V7X_LANES), axis=-1, keepdims=True)
    rest = jnp.where(lane == e1, neg, logits)
    m2 = jnp.max(rest, axis=-1, keepdims=True)
    e2 = jnp.min(jnp.where(rest == m2, lane, V7X_LANES), axis=-1, keepdims=True)
    ex = jnp.exp(m2 - m1)
    w1 = 1.0 / (1.0 + ex)
    w2 = ex * w1

    chosen = (lane == e1) | (lane == e2)
    member = jnp.where(chosen, 1.0, 0.0)
    rank = _dot(tri_ref[...], member.astype(BF16)) + seen_ref[...]
    seen_ref[...] = seen_ref[...] + jnp.sum(member, axis=0, keepdims=True)
    counts_ref[...] = jnp.broadcast_to(seen_ref[...], counts_ref.shape)
    r1 = jnp.sum(jnp.where(lane == e1, rank, 0.0), axis=-1, keepdims=True)
    r2 = jnp.sum(jnp.where(lane == e2, rank, 0.0), axis=-1, keepdims=True)

    record = jnp.zeros(logits.shape, F32)
    for slot, val in ((ROUTE_W1, w1), (ROUTE_W2, w2), (ROUTE_E1, e1.astype(F32)),
                      (ROUTE_E2, e2.astype(F32)), (ROUTE_R1, r1), (ROUTE_R2, r2)):
        record = jnp.where(lane == slot, val, record)
    route_ref[...] = record
    route_t_ref[...] = record.T[0:V7X_SUBLANES, :]


def _xattn(h, g, wq, kt, v, wo, seq, mem_len, router=None):
    n, d = h.shape
    tm = TOKEN_TILE
    tpb = seq // tm
    row = lambda i: (i, 0)
    in_specs = [pl.BlockSpec((tm, d), row), _resident(g.shape), _resident(wq.shape),
                pl.BlockSpec((d, mem_len), lambda i: (i // tpb, 0)),
                pl.BlockSpec((mem_len, d), lambda i: (i // tpb, 0)),
                _resident(wo.shape)]
    if router is None:
        return pl.pallas_call(
            _xattn_kernel, grid=(n // tm,), in_specs=in_specs,
            out_specs=pl.BlockSpec((tm, d), row),
            out_shape=jax.ShapeDtypeStruct((n, d), F32),
            compiler_params=_params(), name="xattn",
        )(h, g, wq, kt, v, wo)
    g_ffn, w_router, tri = router
    return pl.pallas_call(
        _xattn_router_kernel, grid=(n // tm,),
        in_specs=in_specs + [_resident(g_ffn.shape), _resident(w_router.shape), _resident(tri.shape)],
        out_specs=[pl.BlockSpec((tm, d), row), pl.BlockSpec((tm * ROW_CHUNKS, V7X_LANES), row),
                   pl.BlockSpec((tm, V7X_LANES), row),
                   pl.BlockSpec((V7X_SUBLANES, tm), lambda i: (0, i)),
                   pl.BlockSpec((V7X_SUBLANES, V7X_LANES), lambda i: (0, 0))],
        out_shape=[jax.ShapeDtypeStruct((n, d), F32),
                   jax.ShapeDtypeStruct((n * ROW_CHUNKS, V7X_LANES), F32),
                   jax.ShapeDtypeStruct((n, V7X_LANES), F32),
                   jax.ShapeDtypeStruct((V7X_SUBLANES, n), F32),
                   jax.ShapeDtypeStruct((V7X_SUBLANES, V7X_LANES), F32)],
        scratch_shapes=[pltpu.VMEM((1, V7X_LANES), F32)],
        compiler_params=_params(), name="xattn_router",
    )(h, g, wq, kt, v, wo, g_ffn, w_router, tri)


def _swiglu(hn, wg_ref, wu_ref, wd_ref):
    acc = None
    for start, size in FF_CHUNKS:
        gate = _dot(hn, wg_ref[:, start:start + size])
        up = _dot(hn, wu_ref[:, start:start + size])
        act = (gate * _sigmoid(gate) * up).astype(BF16)
        part = _dot(act, wd_ref[start:start + size, :])
        acc = part if acc is None else acc + part
    return acc


def _ffn_kernel(h_ref, g_ref, wg_ref, wu_ref, wd_ref, o_ref):
    h = h_ref[...]
    hn = _rms(h, g_ref[...]).astype(BF16)
    o_ref[...] = h + _swiglu(hn, wg_ref, wu_ref, wd_ref)


def _ffn(h, g, wg, wu, wd):
    n, d = h.shape
    tm = TOKEN_TILE
    row = lambda i: (i, 0)
    return pl.pallas_call(
        _ffn_kernel, grid=(n // tm,),
        in_specs=[pl.BlockSpec((tm, d), row), _resident(g.shape), _resident(wg.shape),
                  _resident(wu.shape), _resident(wd.shape)],
        out_specs=pl.BlockSpec((tm, d), row),
        out_shape=jax.ShapeDtypeStruct((n, d), F32),
        compiler_params=_params(), name="ffn",
    )(h, g, wg, wu, wd)


def _gmlp_kernel(h_ref, g_ref, w_in_ref, lng_ref, lnb_ref, ws_ref, bias_ref, w_out_ref, o_ref):
    tm = h_ref.shape[0]
    nblk = tm // C_BLOCK
    h = h_ref[...]
    hn = _rms(h, g_ref[...]).astype(BF16)
    u = _gelu_tanh(_dot(hn, w_in_ref[:, 0:C_WIDTH]))
    v = _gelu_tanh(_dot(hn, w_in_ref[:, C_WIDTH:2 * C_WIDTH]))
    v = _layer_norm(v, lng_ref[...], lnb_ref[...]).astype(BF16)

    qi = lax.broadcasted_iota(I32, (C_BLOCK, C_BLOCK), 0) // CHUNK
    kj = lax.broadcasted_iota(I32, (C_BLOCK, C_BLOCK), 1) // CHUNK
    causal = kj <= qi
    mixed = []
    for grp in range(C_GROUPS):
        lo, hi = grp * C_GROUP_DIM, (grp + 1) * C_GROUP_DIM
        w = jnp.where(causal, ws_ref[grp], 0.0).astype(BF16)
        rhs = jnp.concatenate([v[b * C_BLOCK:(b + 1) * C_BLOCK, lo:hi] for b in range(nblk)], axis=1)
        mixed.append(_dot(w, rhs))
    rows = []
    for b in range(nblk):
        blk = jnp.concatenate([m[:, b * C_GROUP_DIM:(b + 1) * C_GROUP_DIM] for m in mixed], axis=1)
        rows.append(blk + bias_ref[...])
    s = jnp.concatenate(rows, axis=0)
    o_ref[...] = h + _dot((u * s).astype(BF16), w_out_ref[...])


def _gmlp(h, g, w_in, lng, lnb, ws, bias, w_out):
    n, d = h.shape
    tm = TOKEN_TILE
    row = lambda i: (i, 0)
    return pl.pallas_call(
        _gmlp_kernel, grid=(n // tm,),
        in_specs=[pl.BlockSpec((tm, d), row), _resident(g.shape), _resident(w_in.shape),
                  _resident(lng.shape), _resident(lnb.shape), _resident(ws.shape),
                  _resident(bias.shape), _resident(w_out.shape)],
        out_specs=pl.BlockSpec((tm, d), row),
        out_shape=jax.ShapeDtypeStruct((n, d), F32),
        compiler_params=_params(), name="gmlp",
    )(h, g, w_in, lng, lnb, ws, bias, w_out)


def _row_tile(ref, r):
    return ref.at[pl.ds(pl.multiple_of(r * ROW_CHUNKS, ROW_CHUNKS), ROW_CHUNKS), :]


def _gather_rows(src_hbm, rows_ref, n_rows, dst, sem):
    for i in range(n_rows):
        r = rows_ref[i // V7X_LANES, i % V7X_LANES]
        pltpu.make_async_copy(_row_tile(src_hbm, r), dst.at[i * ROW_CHUNKS:(i + 1) * ROW_CHUNKS, :],
                              sem).start(priority=i % DMA_PRIORITIES)


def _wait_rows(src_hbm, dst, sem):
    pltpu.make_async_copy(src_hbm.at[pl.ds(0, dst.shape[0]), :], dst, sem).wait()


def _expert_kernel(tile_expert_ref, n_tiles_ref, nxt_ref, cur_ref, hn_hbm, wg_ref, wu_ref, wd_ref,
                   y_ref, xbuf0, xbuf1, gsem):
    j = pl.program_id(0)
    n_tiles = n_tiles_ref[0]
    tm = TOKEN_TILE
    xbufs = (xbuf0, xbuf1)

    @pl.when(j == 0)
    def _():
        _gather_rows(hn_hbm, cur_ref, tm, xbuf0, gsem.at[0])

    def tile(slot):
        _wait_rows(hn_hbm, xbufs[slot], gsem.at[slot])
        x = _load_row_tiles(xbufs[slot], 0, tm).astype(BF16)
        _gather_rows(hn_hbm, nxt_ref, tm, xbufs[1 - slot], gsem.at[1 - slot])
        _store_row_tiles(y_ref, _swiglu(x, wg_ref, wu_ref, wd_ref))

        @pl.when(j == n_tiles - 1)
        def _():
            _wait_rows(hn_hbm, xbufs[1 - slot], gsem.at[1 - slot])

    for slot in range(2):
        pl.when((j < n_tiles) & (j % 2 == slot))(functools.partial(tile, slot))

    @pl.when(j >= n_tiles)
    def _():
        y_ref[...] = jnp.zeros(y_ref.shape, F32)


def _experts(hn_tiles, tile_tokens, tile_expert, n_tiles, wg, wu, wd):
    tm = TOKEN_TILE
    max_tiles = tile_expert.shape[0]
    code_rows = tm // V7X_LANES
    expert = lambda j, te, nt: (te[j], 0, 0)
    grid_spec = pltpu.PrefetchScalarGridSpec(
        num_scalar_prefetch=2, grid=(max_tiles,),
        in_specs=[pl.BlockSpec((None, code_rows, V7X_LANES),
                               lambda j, te, nt: (jnp.minimum(j + 1, max_tiles - 1), 0, 0),
                               memory_space=pltpu.SMEM),
                  pl.BlockSpec((None, code_rows, V7X_LANES), lambda j, te, nt: (j, 0, 0),
                               memory_space=pltpu.SMEM),
                  pl.BlockSpec(memory_space=pl.ANY),
                  pl.BlockSpec((None, D_MODEL, D_FF), expert),
                  pl.BlockSpec((None, D_MODEL, D_FF), expert),
                  pl.BlockSpec((None, D_FF, D_MODEL), expert)],
        out_specs=pl.BlockSpec((tm * ROW_CHUNKS, V7X_LANES), lambda j, te, nt: (j, 0)),
        scratch_shapes=[pltpu.VMEM((tm * ROW_CHUNKS, V7X_LANES), F32),
                        pltpu.VMEM((tm * ROW_CHUNKS, V7X_LANES), F32),
                        pltpu.SemaphoreType.DMA((2,))])
    return pl.pallas_call(
        _expert_kernel, grid_spec=grid_spec,
        out_shape=jax.ShapeDtypeStruct((max_tiles * tm * ROW_CHUNKS, V7X_LANES), F32),
        compiler_params=_params(), name="experts",
    )(tile_expert, n_tiles, tile_tokens, tile_tokens, hn_tiles, wg, wu, wd)


def _tile_plan(route_t, counts, n_tokens):
    tm = TOKEN_TILE
    max_tiles = TOP_K * n_tokens // tm + N_EXPERTS
    counts = counts.astype(I32)
    tiles_e = (counts + tm - 1) // tm
    tile_end = jnp.cumsum(tiles_e)
    row_start = (tile_end - tiles_e) * tm
    n_tiles = tile_end[-1:]
    tile_ids = jnp.arange(max_tiles, dtype=I32)
    tile_expert = jnp.minimum(jnp.sum(tile_end[None, :] <= tile_ids[:, None], axis=1),
                              N_EXPERTS - 1).astype(I32)

    def row_of(e_row, r_row):
        e = route_t[e_row].astype(I32)
        start = jnp.zeros_like(e)
        for k in range(N_EXPERTS):
            start = jnp.where(e == k, row_start[k], start)
        return start + route_t[r_row].astype(I32)

    pos1 = row_of(ROUTE_E1, ROUTE_R1)
    pos2 = row_of(ROUTE_E2, ROUTE_R2)
    tok = jnp.arange(n_tokens, dtype=I32)
    pad_j = jnp.arange(tm, dtype=I32)[None, :]
    pad_used = pad_j < (tiles_e * tm - counts)[:, None]
    pad_pos = jnp.where(pad_used, (row_start + counts)[:, None] + pad_j,
                        max_tiles * tm + jnp.arange(N_EXPERTS, dtype=I32)[:, None] * tm + pad_j)
    pad_tok = jnp.broadcast_to(pad_j, (N_EXPERTS, tm))
    keys = jnp.concatenate([pos1, pos2, pad_pos.reshape(-1)])
    vals = jnp.concatenate([tok, tok, pad_tok.reshape(-1)])
    tok_bits = n_tokens.bit_length() - 1
    assert (max_tiles + N_EXPERTS) * tm <= 1 << (32 - tok_bits)
    packed = jnp.sort((keys.astype(jnp.uint32) << tok_bits) | vals.astype(jnp.uint32))
    tile_tokens = (packed & (n_tokens - 1)).astype(I32)
    code_rows = tm // V7X_LANES
    token_rows = jnp.concatenate([pos1.reshape(n_tokens // tm, code_rows, V7X_LANES),
                                  pos2.reshape(n_tokens // tm, code_rows, V7X_LANES)], axis=1)
    return tile_tokens.reshape(max_tiles, code_rows, V7X_LANES), tile_expert, n_tiles, token_rows


def _combine_kernel(nxt_ref, cur_ref, h_ref, route_ref, y_hbm, gf_ref, o_ref, ybuf0, ybuf1, gsem):
    i = pl.program_id(0)
    tm = h_ref.shape[0]
    ybufs = (ybuf0, ybuf1)

    @pl.when(i == 0)
    def _():
        _gather_rows(y_hbm, cur_ref, TOP_K * tm, ybuf0, gsem.at[0])

    def tile(slot):
        _wait_rows(y_hbm, ybufs[slot], gsem.at[slot])

        @pl.when(i + 1 < pl.num_programs(0))
        def _():
            _gather_rows(y_hbm, nxt_ref, TOP_K * tm, ybufs[1 - slot], gsem.at[1 - slot])

        w1 = route_ref[:, ROUTE_W1:ROUTE_W1 + 1]
        w2 = route_ref[:, ROUTE_W2:ROUTE_W2 + 1]
        y = (w1 * _load_row_tiles(ybufs[slot], 0, tm)
             + w2 * _load_row_tiles(ybufs[slot], tm * ROW_CHUNKS, tm))
        o_ref[...] = _rms(h_ref[...] + y, gf_ref[...])

    for slot in range(2):
        pl.when(i % 2 == slot)(functools.partial(tile, slot))


def _combine(h, route, y_tiles, token_rows, g_final):
    n, d = h.shape
    tm = TOKEN_TILE
    steps = n // tm
    row = lambda i: (i, 0)
    rows_block = (None,) + token_rows.shape[1:]
    return pl.pallas_call(
        _combine_kernel, grid=(steps,),
        in_specs=[pl.BlockSpec(rows_block, lambda i: (jnp.minimum(i + 1, steps - 1), 0, 0),
                               memory_space=pltpu.SMEM),
                  pl.BlockSpec(rows_block, lambda i: (i, 0, 0), memory_space=pltpu.SMEM),
                  pl.BlockSpec((tm, d), row), pl.BlockSpec((tm, V7X_LANES), row),
                  pl.BlockSpec(memory_space=pl.ANY), _resident(g_final.shape)],
        out_specs=pl.BlockSpec((tm, d), row),
        out_shape=jax.ShapeDtypeStruct((n, d), F32),
        scratch_shapes=[pltpu.VMEM((TOP_K * tm * ROW_CHUNKS, V7X_LANES), F32),
                        pltpu.VMEM((TOP_K * tm * ROW_CHUNKS, V7X_LANES), F32),
                        pltpu.SemaphoreType.DMA((2,))],
        compiler_params=_params(), name="combine",
    )(token_rows, token_rows, h, route, y_tiles, g_final)


def kernel(x, mem, norm_mix_g, norm_xattn_g, norm_mem_g, norm_ffn_g, final_norm_g, xa_w_q, xa_w_k, xa_w_v, xa_w_o, cv_w_in, cv_a_conv_w, cv_a_conv_b, cv_a_ln_g, cv_a_ln_b, cv_b_conv_w, cv_w_out, ffn_w_gate, ffn_w_up, ffn_w_down, sg_w_in, sg_ln_g, sg_ln_b, sg_w_s, sg_b_s, sg_w_out, moe_w_router, moe_w_gate, moe_w_up, moe_w_down):
    batch, seq, d = x.shape
    mem_len = mem.shape[1]
    n = batch * seq
    assert n & (n - 1) == 0 and seq % TOKEN_TILE == 0
    vec = lambda a: a.reshape(1, -1)
    bf = lambda a: a.astype(BF16)

    h = x.reshape(n, d)
    mem2d = mem.reshape(batch * mem_len, d)

    h = _conv_mixer(h, vec(norm_mix_g[0]), bf(cv_w_in[0]), cv_a_conv_w[0], vec(cv_a_conv_b[0]),
                    vec(cv_a_ln_g[0]), vec(cv_a_ln_b[0]), cv_b_conv_w[0], bf(cv_w_out[0]), seq)
    kt, v = _mem_kv(mem2d, vec(norm_mem_g[0]), bf(xa_w_k[0]), bf(xa_w_v[0]), batch, mem_len)
    h = _xattn(h, vec(norm_xattn_g[0]), bf(xa_w_q[0]), kt, v, bf(xa_w_o[0]), seq, mem_len)
    h = _ffn(h, vec(norm_ffn_g[0]), bf(ffn_w_gate[0]), bf(ffn_w_up[0]), bf(ffn_w_down[0]))

    bias = jnp.repeat(sg_b_s[0].T, C_GROUP_DIM, axis=1)
    h = _gmlp(h, vec(norm_mix_g[1]), bf(sg_w_in[0]), vec(sg_ln_g[0]), vec(sg_ln_b[0]), sg_w_s[0],
              bias, bf(sg_w_out[0]))
    kt, v = _mem_kv(mem2d, vec(norm_mem_g[1]), bf(xa_w_k[1]), bf(xa_w_v[1]), batch, mem_len)
    w_r = moe_w_router[0]
    w_r_hi = bf(w_r)
    w_r_lo = bf(w_r - w_r_hi.astype(F32))
    w_router = jnp.pad(jnp.concatenate([w_r_hi, w_r_lo], axis=1),
                       ((0, 0), (0, V7X_LANES - 2 * N_EXPERTS)))
    tri = jnp.tri(TOKEN_TILE, k=-1, dtype=BF16)
    h, hn_tiles, route, route_t, counts = _xattn(
        h, vec(norm_xattn_g[1]), bf(xa_w_q[1]), kt, v, bf(xa_w_o[1]), seq, mem_len,
        router=(vec(norm_ffn_g[1]), w_router, tri))
    tile_tokens, tile_expert, n_tiles, token_rows = _tile_plan(route_t, counts[0, :N_EXPERTS], n)
    y_tiles = _experts(hn_tiles, tile_tokens, tile_expert, n_tiles, bf(moe_w_gate[0]),
                       bf(moe_w_up[0]), bf(moe_w_down[0]))
    out = _combine(h, route, y_tiles, token_rows, vec(final_norm_g))
    return out.reshape(batch, seq, d)
```

```python
import functools

import jax
import jax.numpy as jnp
from jax import lax
from jax.experimental import pallas as pl
from jax.experimental.pallas import tpu as pltpu

F32 = jnp.float32
BF16 = jnp.bfloat16
I32 = jnp.int32

D_MODEL = 1024
A_WIDTH = 512
A_TAPS = 31
B_WIDTH = 512
B_TAPS = 3
C_WIDTH = 1024
C_GROUPS = 8
C_GROUP_DIM = 128
C_BLOCK = 128
CHUNK = 64
XA_HEADS = 4
XA_HEAD_DIM = 256
D_FF = 2816
N_EXPERTS = 8
TOP_K = 2
RMS_EPS = 1e-6
LN_EPS = 1e-5

V7X_LANES = 128
V7X_SUBLANES = 8
BF16_SUBLANES = 16
V7X_VMEM_LIMIT_BYTES = 60000 * 1024
DMA_PRIORITIES = 2

TOKEN_TILE = 512
CONV_HALO = 32
CONV_ROW_CHUNK = 32
FF_CHUNKS = ((0, 512), (512, 512), (1024, 512), (1536, 512), (2048, 512), (2560, 256))
ROW_CHUNKS = D_MODEL // V7X_LANES
assert ROW_CHUNKS == V7X_SUBLANES
ROUTE_W1, ROUTE_W2, ROUTE_E1, ROUTE_E2, ROUTE_R1, ROUTE_R2 = range(6)


def _params(n_axes=1):
    return pltpu.CompilerParams(dimension_semantics=("arbitrary",) * n_axes,
                                vmem_limit_bytes=V7X_VMEM_LIMIT_BYTES)


def _resident(shape):
    return pl.BlockSpec(shape, lambda *_: (0,) * len(shape), pipeline_mode=pl.Buffered(1))


def _rms(x, g):
    return x * lax.rsqrt(jnp.mean(x * x, axis=-1, keepdims=True) + RMS_EPS) * g


def _layer_norm(x, g, b):
    mu = jnp.mean(x, axis=-1, keepdims=True)
    xc = x - mu
    var = jnp.mean(xc * xc, axis=-1, keepdims=True)
    return xc * lax.rsqrt(var + LN_EPS) * g + b


def _sigmoid(x):
    return 1.0 / (1.0 + jnp.exp(-x))


def _gelu_tanh(x):
    return 0.5 * x * (1.0 + jnp.tanh(0.7978845608028654 * (x + 0.044715 * (x * x * x))))


def _dot(a, b):
    return jnp.dot(a, b, preferred_element_type=F32)


def _store_row_tiles(ref, x):
    rows = x.shape[0]
    for s in range(ROW_CHUNKS):
        ref[pl.ds(s, rows, stride=ROW_CHUNKS), :] = x[:, s * V7X_LANES:(s + 1) * V7X_LANES]


def _load_row_tiles(ref, base, rows):
    return jnp.concatenate(
        [ref[pl.ds(base + s, rows, stride=ROW_CHUNKS), :] for s in range(ROW_CHUNKS)], axis=1)


def _with_casts(body, n_in, n_out, n_cast):
    def kernel(*refs):
        a, b, c = n_in + n_cast, n_in + n_cast + n_out, n_in + 2 * n_cast + n_out
        body(*refs[:n_in], *refs[a:b], *refs[c:])
        for src_ref, dst_ref in zip(refs[n_in:a], refs[b:c]):
            dst_ref[...] = src_ref[...].astype(BF16)
    return kernel


def _cast_specs(weights, steps):
    in_specs, out_specs, out_shapes = [], [], []
    for w in weights:
        rows, cols = w.shape
        block = next(r for r in range(BF16_SUBLANES, rows + 1, BF16_SUBLANES)
                     if rows % r == 0 and r * steps >= rows)
        last = rows // block - 1
        spec = pl.BlockSpec((block, cols), lambda i, last=last: (jnp.minimum(i, last), 0))
        in_specs.append(spec)
        out_specs.append(spec)
        out_shapes.append(jax.ShapeDtypeStruct(w.shape, BF16))
    return in_specs, out_specs, out_shapes


def _conv_mixer_kernel(tiles_per_batch, x_ref, g_ref, w_in_ref, aw_ref, ab_ref, lng_ref, lnb_ref,
                       bw_ref, w_out_ref, o_ref, ext_ref, bext_ref, apost_ref):
    tm = x_ref.shape[0]
    a_tiles = A_WIDTH // V7X_LANES
    b_tiles = B_WIDTH // V7X_LANES
    lanes = lambda t: slice(t * V7X_LANES, (t + 1) * V7X_LANES)

    @pl.when(pl.program_id(0) % tiles_per_batch == 0)
    def _():
        ext_ref[:, 0:CONV_HALO, :] = jnp.zeros((a_tiles, CONV_HALO, V7X_LANES), F32)
        bext_ref[:, 0:V7X_SUBLANES, :] = jnp.zeros((b_tiles, V7X_SUBLANES, V7X_LANES), F32)

    x = x_ref[...]
    hn = _rms(x, g_ref[...]).astype(BF16)

    a_val = _dot(hn, w_in_ref[:, 0:A_WIDTH])
    a_gate = _dot(hn, w_in_ref[:, A_WIDTH:2 * A_WIDTH])
    a = a_val * _sigmoid(a_gate)
    for t in range(a_tiles):
        ext_ref[t, CONV_HALO:CONV_HALO + tm, :] = a[:, lanes(t)]

    base = 2 * A_WIDTH
    g_b = _dot(hn, w_in_ref[:, base:base + B_WIDTH])
    g_c = _dot(hn, w_in_ref[:, base + B_WIDTH:base + 2 * B_WIDTH])
    h_b = _dot(hn, w_in_ref[:, base + 2 * B_WIDTH:base + 3 * B_WIDTH])
    u = g_c * h_b
    conv_b = []
    for t in range(b_tiles):
        bext_ref[t, V7X_SUBLANES:V7X_SUBLANES + tm, :] = u[:, lanes(t)]
        acc = jnp.zeros((tm, V7X_LANES), F32)
        for k in range(B_TAPS):
            off = V7X_SUBLANES - (B_TAPS - 1) + k
            acc = acc + bw_ref[k:k + 1, lanes(t)] * bext_ref[t, off:off + tm, :]
        conv_b.append(acc)
        bext_ref[t, 0:V7X_SUBLANES, :] = bext_ref[t, tm:tm + V7X_SUBLANES, :]
    b_out = (g_b * jnp.concatenate(conv_b, axis=1)).astype(BF16)

    first = CONV_HALO - (A_TAPS - 1)
    for r0 in range(0, tm, CONV_ROW_CHUNK):
        parts = []
        for t in range(a_tiles):
            acc = jnp.zeros((CONV_ROW_CHUNK, V7X_LANES), F32) + ab_ref[:, lanes(t)]
            for k in range(A_TAPS):
                start = r0 + first + k
                acc = acc + aw_ref[k:k + 1, lanes(t)] * ext_ref[t, start:start + CONV_ROW_CHUNK, :]
            parts.append(acc)
        y = _layer_norm(jnp.concatenate(parts, axis=1), lng_ref[...], lnb_ref[...])
        apost_ref[r0:r0 + CONV_ROW_CHUNK, :] = (y * _sigmoid(y)).astype(BF16)
    for t in range(a_tiles):
        ext_ref[t, 0:CONV_HALO, :] = ext_ref[t, tm:tm + CONV_HALO, :]

    o_ref[...] = (x + _dot(apost_ref[...], w_out_ref[0:A_WIDTH, :])
                  + _dot(b_out, w_out_ref[A_WIDTH:A_WIDTH + B_WIDTH, :]))


def _conv_mixer(x, g, w_in, aw, ab, lng, lnb, bw, w_out, seq, casts):
    n, d = x.shape
    tm = TOKEN_TILE
    row = lambda i: (i, 0)
    c_in, c_out, c_shapes = _cast_specs(casts, n // tm)
    body = functools.partial(_conv_mixer_kernel, seq // tm)
    return pl.pallas_call(
        _with_casts(body, 9, 1, len(casts)),
        grid=(n // tm,),
        in_specs=[pl.BlockSpec((tm, d), row), _resident(g.shape), _resident(w_in.shape),
                  _resident(aw.shape), _resident(ab.shape), _resident(lng.shape), _resident(lnb.shape),
                  _resident(bw.shape), _resident(w_out.shape)] + c_in,
        out_specs=[pl.BlockSpec((tm, d), row)] + c_out,
        out_shape=[jax.ShapeDtypeStruct((n, d), F32)] + c_shapes,
        scratch_shapes=[pltpu.VMEM((A_WIDTH // V7X_LANES, tm + CONV_HALO, V7X_LANES), F32),
                        pltpu.VMEM((B_WIDTH // V7X_LANES, tm + V7X_SUBLANES, V7X_LANES), F32),
                        pltpu.VMEM((tm, A_WIDTH), BF16)],
        compiler_params=_params(),
        name="conv_mixer",
    )(x, g, w_in, aw, ab, lng, lnb, bw, w_out, *casts)


def _mem_kv_kernel(mem_ref, g_ref, wk_ref, wv_ref, kt_ref, v_ref):
    mn = _rms(mem_ref[...], g_ref[...]).astype(BF16)
    kt_ref[...] = _dot(mn, wk_ref[...]).T.astype(BF16)
    v_ref[...] = _dot(mn, wv_ref[...]).astype(BF16)


def _mem_kv(mem2d, g, wk, wv, batch, mem_len):
    d = mem2d.shape[1]
    return pl.pallas_call(
        _mem_kv_kernel,
        grid=(batch,),
        in_specs=[pl.BlockSpec((mem_len, d), lambda b: (b, 0)), _resident(g.shape),
                  _resident(wk.shape), _resident(wv.shape)],
        out_specs=[pl.BlockSpec((d, mem_len), lambda b: (b, 0)),
                   pl.BlockSpec((mem_len, d), lambda b: (b, 0))],
        out_shape=[jax.ShapeDtypeStruct((batch * d, mem_len), BF16),
                   jax.ShapeDtypeStruct((batch * mem_len, d), BF16)],
        compiler_params=_params(),
        name="mem_kv",
    )(mem2d, g, wk, wv)


def _attend(h, g_ref, wq_ref, kt_ref, v_ref, wo_ref):
    hn = _rms(h, g_ref[...]).astype(BF16)
    q = (_dot(hn, wq_ref[...]) * (XA_HEAD_DIM ** -0.5)).astype(BF16)
    heads = []
    for hd in range(XA_HEADS):
        lo, hi = hd * XA_HEAD_DIM, (hd + 1) * XA_HEAD_DIM
        s = _dot(q[:, lo:hi], kt_ref[lo:hi, :])
        p = jnp.exp(s - jnp.max(s, axis=-1, keepdims=True))
        p = p * (1.0 / jnp.sum(p, axis=-1, keepdims=True))
        heads.append(_dot(p.astype(BF16), v_ref[:, lo:hi]))
    o = jnp.concatenate(heads, axis=-1).astype(BF16)
    return h + _dot(o, wo_ref[...])


def _xattn_kernel(h_ref, g_ref, wq_ref, kt_ref, v_ref, wo_ref, o_ref):
    o_ref[...] = _attend(h_ref[...], g_ref, wq_ref, kt_ref, v_ref, wo_ref)


def _xattn_router_kernel(h_ref, g_ref, wq_ref, kt_ref, v_ref, wo_ref, gf_ref, wr_ref, tri_ref,
                         o_ref, hn_ref, route_ref, route_t_ref, counts_ref, seen_ref):
    @pl.when(pl.program_id(0) == 0)
    def _():
        seen_ref[...] = jnp.zeros(seen_ref.shape, F32)

    h = _attend(h_ref[...], g_ref, wq_ref, kt_ref, v_ref, wo_ref)
    o_ref[...] = h
    hn = _rms(h, gf_ref[...])
    _store_row_tiles(hn_ref, hn)

    hn_hi = hn.astype(BF16)
    hn_lo = (hn - hn_hi.astype(F32)).astype(BF16)
    p_hi = _dot(hn_hi, wr_ref[...])
    p_lo = _dot(hn_lo, wr_ref[...])
    logits = (pltpu.roll(p_hi, V7X_LANES - N_EXPERTS, axis=1) + p_lo) + p_hi

    lane = lax.broadcasted_iota(I32, logits.shape, 1)
    neg = jnp.float32(-jnp.inf)
    logits = jnp.where(lane < N_EXPERTS, logits, neg)
    m1 = jnp.max(logits, axis=-1, keepdims=True)
    e1 = jnp.min(jnp.where(logits == m1, lane, V7X_LANES), axis=-1, keepdims=True)
    rest = jnp.where(lane == e1, neg, logits)
    m2 = jnp.max(rest, axis=-1, keepdims=True)
    e2 = jnp.min(jnp.where(rest == m2, lane, V7X_LANES), axis=-1, keepdims=True)
    ex = jnp.exp(m2 - m1)
    w1 = 1.0 / (1.0 + ex)
    w2 = ex * w1

    chosen = (lane == e1) | (lane == e2)
    member = jnp.where(chosen, 1.0, 0.0)
    rank = _dot(tri_ref[...], member.astype(BF16)) + seen_ref[...]
    seen_ref[...] = seen_ref[...] + jnp.sum(member, axis=0, keepdims=True)
    counts_ref[...] = jnp.broadcast_to(seen_ref[...], counts_ref.shape)
    r1 = jnp.sum(jnp.where(lane == e1, rank, 0.0), axis=-1, keepdims=True)
    r2 = jnp.sum(jnp.where(lane == e2, rank, 0.0), axis=-1, keepdims=True)

    record = jnp.zeros(logits.shape, F32)
    for slot, val in ((ROUTE_W1, w1), (ROUTE_W2, w2), (ROUTE_E1, e1.astype(F32)),
                      (ROUTE_E2, e2.astype(F32)), (ROUTE_R1, r1), (ROUTE_R2, r2)):
        record = jnp.where(lane == slot, val, record)
    route_ref[...] = record
    route_t_ref[...] = record.T[0:V7X_SUBLANES, :]


def _xattn(h, g, wq, kt, v, wo, seq, mem_len, router=None, casts=()):
    n, d = h.shape
    tm = TOKEN_TILE
    tpb = seq // tm
    row = lambda i: (i, 0)
    in_specs = [pl.BlockSpec((tm, d), row), _resident(g.shape), _resident(wq.shape),
                pl.BlockSpec((d, mem_len), lambda i: (i // tpb, 0)),
                pl.BlockSpec((mem_len, d), lambda i: (i // tpb, 0)),
                _resident(wo.shape)]
    if router is None:
        c_in, c_out, c_shapes = _cast_specs(casts, n // tm)
        return pl.pallas_call(
            _with_casts(_xattn_kernel, 6, 1, len(casts)), grid=(n // tm,), in_specs=in_specs + c_in,
            out_specs=[pl.BlockSpec((tm, d), row)] + c_out,
            out_shape=[jax.ShapeDtypeStruct((n, d), F32)] + c_shapes,
            compiler_params=_params(), name="xattn",
        )(h, g, wq, kt, v, wo, *casts)
    g_ffn, w_router, tri = router
    return pl.pallas_call(
        _xattn_router_kernel, grid=(n // tm,),
        in_specs=in_specs + [_resident(g_ffn.shape), _resident(w_router.shape), _resident(tri.shape)],
        out_specs=[pl.BlockSpec((tm, d), row), pl.BlockSpec((tm * ROW_CHUNKS, V7X_LANES), row),
                   pl.BlockSpec((tm, V7X_LANES), row),
                   pl.BlockSpec((V7X_SUBLANES, tm), lambda i: (0, i)),
                   pl.BlockSpec((V7X_SUBLANES, V7X_LANES), lambda i: (0, 0))],
        out_shape=[jax.ShapeDtypeStruct((n, d), F32),
                   jax.ShapeDtypeStruct((n * ROW_CHUNKS, V7X_LANES), F32),
                   jax.ShapeDtypeStruct((n, V7X_LANES), F32),
                   jax.ShapeDtypeStruct((V7X_SUBLANES, n), F32),
                   jax.ShapeDtypeStruct((V7X_SUBLANES, V7X_LANES), F32)],
        scratch_shapes=[pltpu.VMEM((1, V7X_LANES), F32)],
        compiler_params=_params(), name="xattn_router",
    )(h, g, wq, kt, v, wo, g_ffn, w_router, tri)


def _swiglu(hn, wg_ref, wu_ref, wd_ref):
    acc = None
    for start, size in FF_CHUNKS:
        gate = _dot(hn, wg_ref[:, start:start + size])
        up = _dot(hn, wu_ref[:, start:start + size])
        act = (gate * _sigmoid(gate) * up).astype(BF16)
        part = _dot(act, wd_ref[start:start + size, :])
        acc = part if acc is None else acc + part
    return acc


def _ffn_kernel(h_ref, g_ref, wg_ref, wu_ref, wd_ref, o_ref):
    h = h_ref[...]
    hn = _rms(h, g_ref[...]).astype(BF16)
    o_ref[...] = h + _swiglu(hn, wg_ref, wu_ref, wd_ref)


def _ffn(h, g, wg, wu, wd, casts):
    n, d = h.shape
    tm = TOKEN_TILE
    row = lambda i: (i, 0)
    c_in, c_out, c_shapes = _cast_specs(casts, n // tm)
    return pl.pallas_call(
        _with_casts(_ffn_kernel, 5, 1, len(casts)), grid=(n // tm,),
        in_specs=[pl.BlockSpec((tm, d), row), _resident(g.shape), _resident(wg.shape),
                  _resident(wu.shape), _resident(wd.shape)] + c_in,
        out_specs=[pl.BlockSpec((tm, d), row)] + c_out,
        out_shape=[jax.ShapeDtypeStruct((n, d), F32)] + c_shapes,
        compiler_params=_params(), name="ffn",
    )(h, g, wg, wu, wd, *casts)


def _gmlp_kernel(h_ref, g_ref, w_in_ref, lng_ref, lnb_ref, ws_ref, bias_ref, w_out_ref, o_ref):
    tm = h_ref.shape[0]
    nblk = tm // C_BLOCK
    h = h_ref[...]
    hn = _rms(h, g_ref[...]).astype(BF16)
    u = _gelu_tanh(_dot(hn, w_in_ref[:, 0:C_WIDTH]))
    v = _gelu_tanh(_dot(hn, w_in_ref[:, C_WIDTH:2 * C_WIDTH]))
    v = _layer_norm(v, lng_ref[...], lnb_ref[...]).astype(BF16)

    qi = lax.broadcasted_iota(I32, (C_BLOCK, C_BLOCK), 0) // CHUNK
    kj = lax.broadcasted_iota(I32, (C_BLOCK, C_BLOCK), 1) // CHUNK
    causal = kj <= qi
    mixed = []
    for grp in range(C_GROUPS):
        lo, hi = grp * C_GROUP_DIM, (grp + 1) * C_GROUP_DIM
        w = jnp.where(causal, ws_ref[grp], 0.0).astype(BF16)
        rhs = jnp.concatenate([v[b * C_BLOCK:(b + 1) * C_BLOCK, lo:hi] for b in range(nblk)], axis=1)
        mixed.append(_dot(w, rhs))
    rows = []
    for b in range(nblk):
        blk = jnp.concatenate([m[:, b * C_GROUP_DIM:(b + 1) * C_GROUP_DIM] for m in mixed], axis=1)
        rows.append(blk + bias_ref[...])
    s = jnp.concatenate(rows, axis=0)
    o_ref[...] = h + _dot((u * s).astype(BF16), w_out_ref[...])


def _gmlp(h, g, w_in, lng, lnb, ws, bias, w_out):
    n, d = h.shape
    tm = TOKEN_TILE
    row = lambda i: (i, 0)
    return pl.pallas_call(
        _gmlp_kernel, grid=(n // tm,),
        in_specs=[pl.BlockSpec((tm, d), row), _resident(g.shape), _resident(w_in.shape),
                  _resident(lng.shape), _resident(lnb.shape), _resident(ws.shape),
                  _resident(bias.shape), _resident(w_out.shape)],
        out_specs=pl.BlockSpec((tm, d), row),
        out_shape=jax.ShapeDtypeStruct((n, d), F32),
        compiler_params=_params(), name="gmlp",
    )(h, g, w_in, lng, lnb, ws, bias, w_out)


def _row_tile(ref, r):
    return ref.at[pl.ds(pl.multiple_of(r * ROW_CHUNKS, ROW_CHUNKS), ROW_CHUNKS), :]


def _gather_rows(src_hbm, rows_ref, n_rows, dst, sem):
    for i in range(n_rows):
        r = rows_ref[i // V7X_LANES, i % V7X_LANES]
        pltpu.make_async_copy(_row_tile(src_hbm, r), dst.at[i * ROW_CHUNKS:(i + 1) * ROW_CHUNKS, :],
                              sem).start(priority=i % DMA_PRIORITIES)


def _wait_rows(src_hbm, dst, sem):
    pltpu.make_async_copy(src_hbm.at[pl.ds(0, dst.shape[0]), :], dst, sem).wait()


def _expert_kernel(tile_expert_ref, n_tiles_ref, nxt_ref, cur_ref, hn_hbm, wg_ref, wu_ref, wd_ref,
                   y_ref, xbuf0, xbuf1, gsem):
    j = pl.program_id(0)
    n_tiles = n_tiles_ref[0]
    tm = TOKEN_TILE
    xbufs = (xbuf0, xbuf1)

    @pl.when(j == 0)
    def _():
        _gather_rows(hn_hbm, cur_ref, tm, xbuf0, gsem.at[0])

    def tile(slot):
        _wait_rows(hn_hbm, xbufs[slot], gsem.at[slot])
        x = _load_row_tiles(xbufs[slot], 0, tm).astype(BF16)
        _gather_rows(hn_hbm, nxt_ref, tm, xbufs[1 - slot], gsem.at[1 - slot])
        _store_row_tiles(y_ref, _swiglu(x, wg_ref, wu_ref, wd_ref))

        @pl.when(j == n_tiles - 1)
        def _():
            _wait_rows(hn_hbm, xbufs[1 - slot], gsem.at[1 - slot])

    for slot in range(2):
        pl.when((j < n_tiles) & (j % 2 == slot))(functools.partial(tile, slot))

    @pl.when(j >= n_tiles)
    def _():
        y_ref[...] = jnp.zeros(y_ref.shape, F32)


def _experts(hn_tiles, tile_tokens, tile_expert, n_tiles, wg, wu, wd):
    tm = TOKEN_TILE
    max_tiles = tile_expert.shape[0]
    code_rows = tm // V7X_LANES
    expert = lambda j, te, nt: (te[j], 0, 0)
    grid_spec = pltpu.PrefetchScalarGridSpec(
        num_scalar_prefetch=2, grid=(max_tiles,),
        in_specs=[pl.BlockSpec((None, code_rows, V7X_LANES),
                               lambda j, te, nt: (jnp.minimum(j + 1, max_tiles - 1), 0, 0),
                               memory_space=pltpu.SMEM),
                  pl.BlockSpec((None, code_rows, V7X_LANES), lambda j, te, nt: (j, 0, 0),
                               memory_space=pltpu.SMEM),
                  pl.BlockSpec(memory_space=pl.ANY),
                  pl.BlockSpec((None, D_MODEL, D_FF), expert),
                  pl.BlockSpec((None, D_MODEL, D_FF), expert),
                  pl.BlockSpec((None, D_FF, D_MODEL), expert)],
        out_specs=pl.BlockSpec((tm * ROW_CHUNKS, V7X_LANES), lambda j, te, nt: (j, 0)),
        scratch_shapes=[pltpu.VMEM((tm * ROW_CHUNKS, V7X_LANES), F32),
                        pltpu.VMEM((tm * ROW_CHUNKS, V7X_LANES), F32),
                        pltpu.SemaphoreType.DMA((2,))])
    return pl.pallas_call(
        _expert_kernel, grid_spec=grid_spec,
        out_shape=jax.ShapeDtypeStruct((max_tiles * tm * ROW_CHUNKS, V7X_LANES), F32),
        compiler_params=_params(), name="experts",
    )(tile_expert, n_tiles, tile_tokens, tile_tokens, hn_tiles, wg, wu, wd)


def _tile_plan(route_t, counts, n_tokens):
    tm = TOKEN_TILE
    max_tiles = TOP_K * n_tokens // tm + N_EXPERTS
    counts = counts.astype(I32)
    tiles_e = (counts + tm - 1) // tm
    tile_end = jnp.cumsum(tiles_e)
    row_start = (tile_end - tiles_e) * tm
    n_tiles = tile_end[-1:]
    tile_ids = jnp.arange(max_tiles, dtype=I32)
    tile_expert = jnp.minimum(jnp.sum(tile_end[None, :] <= tile_ids[:, None], axis=1),
                              N_EXPERTS - 1).astype(I32)

    def row_of(e_row, r_row):
        e = route_t[e_row].astype(I32)
        start = jnp.zeros_like(e)
        for k in range(N_EXPERTS):
            start = jnp.where(e == k, row_start[k], start)
        return start + route_t[r_row].astype(I32)

    pos1 = row_of(ROUTE_E1, ROUTE_R1)
    pos2 = row_of(ROUTE_E2, ROUTE_R2)
    tok = jnp.arange(n_tokens, dtype=I32)
    pad_j = jnp.arange(tm, dtype=I32)[None, :]
    pad_used = pad_j < (tiles_e * tm - counts)[:, None]
    pad_pos = jnp.where(pad_used, (row_start + counts)[:, None] + pad_j,
                        max_tiles * tm + jnp.arange(N_EXPERTS, dtype=I32)[:, None] * tm + pad_j)
    pad_tok = jnp.broadcast_to(pad_j, (N_EXPERTS, tm))
    keys = jnp.concatenate([pos1, pos2, pad_pos.reshape(-1)])
    vals = jnp.concatenate([tok, tok, pad_tok.reshape(-1)])
    tok_bits = n_tokens.bit_length() - 1
    assert (max_tiles + N_EXPERTS) * tm <= 1 << (32 - tok_bits)
    packed = jnp.sort((keys.astype(jnp.uint32) << tok_bits) | vals.astype(jnp.uint32))
    tile_tokens = (packed & (n_tokens - 1)).astype(I32)
    code_rows = tm // V7X_LANES
    token_rows = jnp.concatenate([pos1.reshape(n_tokens // tm, code_rows, V7X_LANES),
                                  pos2.reshape(n_tokens // tm, code_rows, V7X_LANES)], axis=1)
    return tile_tokens.reshape(max_tiles, code_rows, V7X_LANES), tile_expert, n_tiles, token_rows


def _combine_kernel(nxt_ref, cur_ref, h_ref, route_ref, y_hbm, gf_ref, o_ref, ybuf0, ybuf1, gsem):
    i = pl.program_id(0)
    tm = h_ref.shape[0]
    ybufs = (ybuf0, ybuf1)

    @pl.when(i == 0)
    def _():
        _gather_rows(y_hbm, cur_ref, TOP_K * tm, ybuf0, gsem.at[0])

    def tile(slot):
        _wait_rows(y_hbm, ybufs[slot], gsem.at[slot])

        @pl.when(i + 1 < pl.num_programs(0))
        def _():
            _gather_rows(y_hbm, nxt_ref, TOP_K * tm, ybufs[1 - slot], gsem.at[1 - slot])

        w1 = route_ref[:, ROUTE_W1:ROUTE_W1 + 1]
        w2 = route_ref[:, ROUTE_W2:ROUTE_W2 + 1]
        y = (w1 * _load_row_tiles(ybufs[slot], 0, tm)
             + w2 * _load_row_tiles(ybufs[slot], tm * ROW_CHUNKS, tm))
        o_ref[...] = _rms(h_ref[...] + y, gf_ref[...])

    for slot in range(2):
        pl.when(i % 2 == slot)(functools.partial(tile, slot))


def _combine(h, route, y_tiles, token_rows, g_final):
    n, d = h.shape
    tm = TOKEN_TILE
    steps = n // tm
    row = lambda i: (i, 0)
    rows_block = (None,) + token_rows.shape[1:]
    return pl.pallas_call(
        _combine_kernel, grid=(steps,),
        in_specs=[pl.BlockSpec(rows_block, lambda i: (jnp.minimum(i + 1, steps - 1), 0, 0),
                               memory_space=pltpu.SMEM),
                  pl.BlockSpec(rows_block, lambda i: (i, 0, 0), memory_space=pltpu.SMEM),
                  pl.BlockSpec((tm, d), row), pl.BlockSpec((tm, V7X_LANES), row),
                  pl.BlockSpec(memory_space=pl.ANY), _resident(g_final.shape)],
        out_specs=pl.BlockSpec((tm, d), row),
        out_shape=jax.ShapeDtypeStruct((n, d), F32),
        scratch_shapes=[pltpu.VMEM((TOP_K * tm * ROW_CHUNKS, V7X_LANES), F32),
                        pltpu.VMEM((TOP_K * tm * ROW_CHUNKS, V7X_LANES), F32),
                        pltpu.SemaphoreType.DMA((2,))],
        compiler_params=_params(), name="combine",
    )(token_rows, token_rows, h, route, y_tiles, g_final)


def kernel(x, mem, norm_mix_g, norm_xattn_g, norm_mem_g, norm_ffn_g, final_norm_g, xa_w_q, xa_w_k, xa_w_v, xa_w_o, cv_w_in, cv_a_conv_w, cv_a_conv_b, cv_a_ln_g, cv_a_ln_b, cv_b_conv_w, cv_w_out, ffn_w_gate, ffn_w_up, ffn_w_down, sg_w_in, sg_ln_g, sg_ln_b, sg_w_s, sg_b_s, sg_w_out, moe_w_router, moe_w_gate, moe_w_up, moe_w_down):
    batch, seq, d = x.shape
    mem_len = mem.shape[1]
    n = batch * seq
    assert n & (n - 1) == 0 and seq % TOKEN_TILE == 0
    vec = lambda a: a.reshape(1, -1)
    bf = lambda a: a.astype(BF16)

    h = x.reshape(n, d)
    mem2d = mem.reshape(batch * mem_len, d)

    h, ffn_wg, ffn_wu, ffn_wd = _conv_mixer(
        h, vec(norm_mix_g[0]), bf(cv_w_in[0]), cv_a_conv_w[0], vec(cv_a_conv_b[0]),
        vec(cv_a_ln_g[0]), vec(cv_a_ln_b[0]), cv_b_conv_w[0], bf(cv_w_out[0]), seq,
        casts=(ffn_w_gate[0], ffn_w_up[0], ffn_w_down[0]))
    kt, v = _mem_kv(mem2d, vec(norm_mem_g[0]), bf(xa_w_k[0]), bf(xa_w_v[0]), batch, mem_len)
    h, sg_in, sg_out, wq1, wk1, wv1, wo1 = _xattn(
        h, vec(norm_xattn_g[0]), bf(xa_w_q[0]), kt, v, bf(xa_w_o[0]), seq, mem_len,
        casts=(sg_w_in[0], sg_w_out[0], xa_w_q[1], xa_w_k[1], xa_w_v[1], xa_w_o[1]))
    flat = lambda w: w.reshape(-1, w.shape[-1])
    h, moe_wg, moe_wu, moe_wd = _ffn(
        h, vec(norm_ffn_g[0]), ffn_wg, ffn_wu, ffn_wd,
        casts=(flat(moe_w_gate[0]), flat(moe_w_up[0]), flat(moe_w_down[0])))

    bias = jnp.repeat(sg_b_s[0].T, C_GROUP_DIM, axis=1)
    h = _gmlp(h, vec(norm_mix_g[1]), sg_in, vec(sg_ln_g[0]), vec(sg_ln_b[0]), sg_w_s[0], bias, sg_out)
    kt, v = _mem_kv(mem2d, vec(norm_mem_g[1]), wk1, wv1, batch, mem_len)
    w_r = moe_w_router[0]
    w_r_hi = bf(w_r)
    w_r_lo = bf(w_r - w_r_hi.astype(F32))
    w_router = jnp.pad(jnp.concatenate([w_r_hi, w_r_lo], axis=1),
                       ((0, 0), (0, V7X_LANES - 2 * N_EXPERTS)))
    tri = jnp.tri(TOKEN_TILE, k=-1, dtype=BF16)
    h, hn_tiles, route, route_t, counts = _xattn(
        h, vec(norm_xattn_g[1]), wq1, kt, v, wo1, seq, mem_len,
        router=(vec(norm_ffn_g[1]), w_router, tri))
    tile_tokens, tile_expert, n_tiles, token_rows = _tile_plan(route_t, counts[0, :N_EXPERTS], n)
    y_tiles = _experts(hn_tiles, tile_tokens, tile_expert, n_tiles,
                       moe_wg.reshape(moe_w_gate[0].shape), moe_wu.reshape(moe_w_up[0].shape),
                       moe_wd.reshape(moe_w_down[0].shape))
    out = _combine(h, route, y_tiles, token_rows, vec(final_norm_g))
    return out.reshape(batch, seq, d)
```

```python
import functools

import jax
import jax.numpy as jnp
from jax import lax
from jax.experimental import pallas as pl
from jax.experimental.pallas import tpu as pltpu

F32 = jnp.float32
BF16 = jnp.bfloat16
I32 = jnp.int32

D_MODEL = 1024
A_WIDTH = 512
A_TAPS = 31
B_WIDTH = 512
B_TAPS = 3
C_WIDTH = 1024
C_GROUPS = 8
C_GROUP_DIM = 128
C_BLOCK = 128
CHUNK = 64
XA_HEADS = 4
XA_HEAD_DIM = 256
D_FF = 2816
N_EXPERTS = 8
TOP_K = 2
RMS_EPS = 1e-6
LN_EPS = 1e-5

V7X_LANES = 128
V7X_SUBLANES = 8
BF16_SUBLANES = 16
V7X_VMEM_LIMIT_BYTES = 60000 * 1024
DMA_PRIORITIES = 2

TOKEN_TILE = 512
WIDE_TOKEN_TILE = 1024
CONV_HALO = 32
CONV_ROW_CHUNK = 32
FF_CHUNKS = ((0, 512), (512, 512), (1024, 512), (1536, 512), (2048, 512), (2560, 256))
ROW_CHUNKS = D_MODEL // V7X_LANES
assert ROW_CHUNKS == V7X_SUBLANES
ROUTE_W1, ROUTE_W2, ROUTE_E1, ROUTE_E2, ROUTE_R1, ROUTE_R2 = range(6)


def _params(n_axes=1):
    return pltpu.CompilerParams(dimension_semantics=("arbitrary",) * n_axes,
                                vmem_limit_bytes=V7X_VMEM_LIMIT_BYTES)


def _resident(shape):
    return pl.BlockSpec(shape, lambda *_: (0,) * len(shape), pipeline_mode=pl.Buffered(1))


def _rms(x, g):
    return x * lax.rsqrt(jnp.mean(x * x, axis=-1, keepdims=True) + RMS_EPS) * g


def _layer_norm(x, g, b):
    mu = jnp.mean(x, axis=-1, keepdims=True)
    xc = x - mu
    var = jnp.mean(xc * xc, axis=-1, keepdims=True)
    return xc * lax.rsqrt(var + LN_EPS) * g + b


def _sigmoid(x):
    return 1.0 / (1.0 + jnp.exp(-x))


def _gelu_tanh(x):
    return 0.5 * x * (1.0 + jnp.tanh(0.7978845608028654 * (x + 0.044715 * (x * x * x))))


def _dot(a, b):
    return jnp.dot(a, b, preferred_element_type=F32)


def _store_row_tiles(ref, x):
    rows = x.shape[0]
    for s in range(ROW_CHUNKS):
        ref[pl.ds(s, rows, stride=ROW_CHUNKS), :] = x[:, s * V7X_LANES:(s + 1) * V7X_LANES]


def _load_row_tiles(ref, base, rows):
    return jnp.concatenate(
        [ref[pl.ds(base + s, rows, stride=ROW_CHUNKS), :] for s in range(ROW_CHUNKS)], axis=1)


def _with_casts(body, n_in, n_out, n_cast):
    def kernel(*refs):
        a, b, c = n_in + n_cast, n_in + n_cast + n_out, n_in + 2 * n_cast + n_out
        body(*refs[:n_in], *refs[a:b], *refs[c:])
        for src_ref, dst_ref in zip(refs[n_in:a], refs[b:c]):
            dst_ref[...] = src_ref[...].astype(BF16)
    return kernel


def _cast_specs(weights, steps):
    in_specs, out_specs, out_shapes = [], [], []
    for w in weights:
        rows, cols = w.shape
        block = next(r for r in range(BF16_SUBLANES, rows + 1, BF16_SUBLANES)
                     if rows % r == 0 and r * steps >= rows)
        last = rows // block - 1
        spec = pl.BlockSpec((block, cols), lambda i, last=last: (jnp.minimum(i, last), 0))
        in_specs.append(spec)
        out_specs.append(spec)
        out_shapes.append(jax.ShapeDtypeStruct(w.shape, BF16))
    return in_specs, out_specs, out_shapes


def _conv_mixer_kernel(tiles_per_batch, x_ref, g_ref, w_in_ref, aw_ref, ab_ref, lng_ref, lnb_ref,
                       bw_ref, w_out_ref, o_ref, ext_ref, bext_ref, apost_ref):
    tm = x_ref.shape[0]
    a_tiles = A_WIDTH // V7X_LANES
    b_tiles = B_WIDTH // V7X_LANES
    lanes = lambda t: slice(t * V7X_LANES, (t + 1) * V7X_LANES)

    @pl.when(pl.program_id(0) % tiles_per_batch == 0)
    def _():
        ext_ref[:, 0:CONV_HALO, :] = jnp.zeros((a_tiles, CONV_HALO, V7X_LANES), F32)
        bext_ref[:, 0:V7X_SUBLANES, :] = jnp.zeros((b_tiles, V7X_SUBLANES, V7X_LANES), F32)

    x = x_ref[...]
    hn = _rms(x, g_ref[...]).astype(BF16)

    a_val = _dot(hn, w_in_ref[:, 0:A_WIDTH])
    a_gate = _dot(hn, w_in_ref[:, A_WIDTH:2 * A_WIDTH])
    a = a_val * _sigmoid(a_gate)
    for t in range(a_tiles):
        ext_ref[t, CONV_HALO:CONV_HALO + tm, :] = a[:, lanes(t)]

    base = 2 * A_WIDTH
    g_b = _dot(hn, w_in_ref[:, base:base + B_WIDTH])
    g_c = _dot(hn, w_in_ref[:, base + B_WIDTH:base + 2 * B_WIDTH])
    h_b = _dot(hn, w_in_ref[:, base + 2 * B_WIDTH:base + 3 * B_WIDTH])
    u = g_c * h_b
    conv_b = []
    for t in range(b_tiles):
        bext_ref[t, V7X_SUBLANES:V7X_SUBLANES + tm, :] = u[:, lanes(t)]
        acc = jnp.zeros((tm, V7X_LANES), F32)
        for k in range(B_TAPS):
            off = V7X_SUBLANES - (B_TAPS - 1) + k
            acc = acc + bw_ref[k:k + 1, lanes(t)] * bext_ref[t, off:off + tm, :]
        conv_b.append(acc)
        bext_ref[t, 0:V7X_SUBLANES, :] = bext_ref[t, tm:tm + V7X_SUBLANES, :]
    b_out = (g_b * jnp.concatenate(conv_b, axis=1)).astype(BF16)

    first = CONV_HALO - (A_TAPS - 1)
    for r0 in range(0, tm, CONV_ROW_CHUNK):
        parts = []
        for t in range(a_tiles):
            acc = jnp.zeros((CONV_ROW_CHUNK, V7X_LANES), F32) + ab_ref[:, lanes(t)]
            for k in range(A_TAPS):
                start = r0 + first + k
                acc = acc + aw_ref[k:k + 1, lanes(t)] * ext_ref[t, start:start + CONV_ROW_CHUNK, :]
            parts.append(acc)
        y = _layer_norm(jnp.concatenate(parts, axis=1), lng_ref[...], lnb_ref[...])
        apost_ref[r0:r0 + CONV_ROW_CHUNK, :] = (y * _sigmoid(y)).astype(BF16)
    for t in range(a_tiles):
        ext_ref[t, 0:CONV_HALO, :] = ext_ref[t, tm:tm + CONV_HALO, :]

    o_ref[...] = (x + _dot(apost_ref[...], w_out_ref[0:A_WIDTH, :])
                  + _dot(b_out, w_out_ref[A_WIDTH:A_WIDTH + B_WIDTH, :]))


def _conv_mixer(x, g, w_in, aw, ab, lng, lnb, bw, w_out, seq, casts):
    n, d = x.shape
    tm = WIDE_TOKEN_TILE
    row = lambda i: (i, 0)
    c_in, c_out, c_shapes = _cast_specs(casts, n // tm)
    body = functools.partial(_conv_mixer_kernel, seq // tm)
    return pl.pallas_call(
        _with_casts(body, 9, 1, len(casts)),
        grid=(n // tm,),
        in_specs=[pl.BlockSpec((tm, d), row), _resident(g.shape), _resident(w_in.shape),
                  _resident(aw.shape), _resident(ab.shape), _resident(lng.shape), _resident(lnb.shape),
                  _resident(bw.shape), _resident(w_out.shape)] + c_in,
        out_specs=[pl.BlockSpec((tm, d), row)] + c_out,
        out_shape=[jax.ShapeDtypeStruct((n, d), F32)] + c_shapes,
        scratch_shapes=[pltpu.VMEM((A_WIDTH // V7X_LANES, tm + CONV_HALO, V7X_LANES), F32),
                        pltpu.VMEM((B_WIDTH // V7X_LANES, tm + V7X_SUBLANES, V7X_LANES), F32),
                        pltpu.VMEM((tm, A_WIDTH), BF16)],
        compiler_params=_params(),
        name="conv_mixer",
    )(x, g, w_in, aw, ab, lng, lnb, bw, w_out, *casts)


def _mem_kv_kernel(mem_ref, g_ref, wk_ref, wv_ref, kt_ref, v_ref):
    mn = _rms(mem_ref[...], g_ref[...]).astype(BF16)
    kt_ref[...] = _dot(mn, wk_ref[...]).T.astype(BF16)
    v_ref[...] = _dot(mn, wv_ref[...]).astype(BF16)


def _mem_kv(mem2d, g, wk, wv, batch, mem_len):
    d = mem2d.shape[1]
    return pl.pallas_call(
        _mem_kv_kernel,
        grid=(batch,),
        in_specs=[pl.BlockSpec((mem_len, d), lambda b: (b, 0)), _resident(g.shape),
                  _resident(wk.shape), _resident(wv.shape)],
        out_specs=[pl.BlockSpec((d, mem_len), lambda b: (b, 0)),
                   pl.BlockSpec((mem_len, d), lambda b: (b, 0))],
        out_shape=[jax.ShapeDtypeStruct((batch * d, mem_len), BF16),
                   jax.ShapeDtypeStruct((batch * mem_len, d), BF16)],
        compiler_params=_params(),
        name="mem_kv",
    )(mem2d, g, wk, wv)


def _attend(h, g_ref, wq_ref, kt_ref, v_ref, wo_ref):
    hn = _rms(h, g_ref[...]).astype(BF16)
    q = (_dot(hn, wq_ref[...]) * (XA_HEAD_DIM ** -0.5)).astype(BF16)
    heads = []
    for hd in range(XA_HEADS):
        lo, hi = hd * XA_HEAD_DIM, (hd + 1) * XA_HEAD_DIM
        s = _dot(q[:, lo:hi], kt_ref[lo:hi, :])
        p = jnp.exp(s - jnp.max(s, axis=-1, keepdims=True))
        p = p * (1.0 / jnp.sum(p, axis=-1, keepdims=True))
        heads.append(_dot(p.astype(BF16), v_ref[:, lo:hi]))
    o = jnp.concatenate(heads, axis=-1).astype(BF16)
    return h + _dot(o, wo_ref[...])


def _xattn_kernel(h_ref, g_ref, wq_ref, kt_ref, v_ref, wo_ref, o_ref):
    o_ref[...] = _attend(h_ref[...], g_ref, wq_ref, kt_ref, v_ref, wo_ref)


def _xattn_router_kernel(h_ref, g_ref, wq_ref, kt_ref, v_ref, wo_ref, gf_ref, wr_ref, tri_ref,
                         o_ref, hn_ref, route_ref, route_t_ref, counts_ref, seen_ref):
    @pl.when(pl.program_id(0) == 0)
    def _():
        seen_ref[...] = jnp.zeros(seen_ref.shape, F32)

    h = _attend(h_ref[...], g_ref, wq_ref, kt_ref, v_ref, wo_ref)
    o_ref[...] = h
    hn = _rms(h, gf_ref[...])
    _store_row_tiles(hn_ref, hn)

    hn_hi = hn.astype(BF16)
    hn_lo = (hn - hn_hi.astype(F32)).astype(BF16)
    p_hi = _dot(hn_hi, wr_ref[...])
    p_lo = _dot(hn_lo, wr_ref[...])
    logits = (pltpu.roll(p_hi, V7X_LANES - N_EXPERTS, axis=1) + p_lo) + p_hi

    lane = lax.broadcasted_iota(I32, logits.shape, 1)
    neg = jnp.float32(-jnp.inf)
    logits = jnp.where(lane < N_EXPERTS, logits, neg)
    m1 = jnp.max(logits, axis=-1, keepdims=True)
    e1 = jnp.min(jnp.where(logits == m1, lane, V7X_LANES), axis=-1, keepdims=True)
    rest = jnp.where(lane == e1, neg, logits)
    m2 = jnp.max(rest, axis=-1, keepdims=True)
    e2 = jnp.min(jnp.where(rest == m2, lane, V7X_LANES), axis=-1, keepdims=True)
    ex = jnp.exp(m2 - m1)
    w1 = 1.0 / (1.0 + ex)
    w2 = ex * w1

    chosen = (lane == e1) | (lane == e2)
    member = jnp.where(chosen, 1.0, 0.0)
    rank = _dot(tri_ref[...], member.astype(BF16)) + seen_ref[...]
    seen_ref[...] = seen_ref[...] + jnp.sum(member, axis=0, keepdims=True)
    counts_ref[...] = jnp.broadcast_to(seen_ref[...], counts_ref.shape)
    r1 = jnp.sum(jnp.where(lane == e1, rank, 0.0), axis=-1, keepdims=True)
    r2 = jnp.sum(jnp.where(lane == e2, rank, 0.0), axis=-1, keepdims=True)

    record = jnp.zeros(logits.shape, F32)
    for slot, val in ((ROUTE_W1, w1), (ROUTE_W2, w2), (ROUTE_E1, e1.astype(F32)),
                      (ROUTE_E2, e2.astype(F32)), (ROUTE_R1, r1), (ROUTE_R2, r2)):
        record = jnp.where(lane == slot, val, record)
    route_ref[...] = record
    route_t_ref[...] = record.T[0:V7X_SUBLANES, :]


def _xattn(h, g, wq, kt, v, wo, seq, mem_len, router=None, casts=()):
    n, d = h.shape
    tm = TOKEN_TILE if router else WIDE_TOKEN_TILE
    tpb = seq // tm
    row = lambda i: (i, 0)
    in_specs = [pl.BlockSpec((tm, d), row), _resident(g.shape), _resident(wq.shape),
                pl.BlockSpec((d, mem_len), lambda i: (i // tpb, 0)),
                pl.BlockSpec((mem_len, d), lambda i: (i // tpb, 0)),
                _resident(wo.shape)]
    if router is None:
        c_in, c_out, c_shapes = _cast_specs(casts, n // tm)
        return pl.pallas_call(
            _with_casts(_xattn_kernel, 6, 1, len(casts)), grid=(n // tm,), in_specs=in_specs + c_in,
            out_specs=[pl.BlockSpec((tm, d), row)] + c_out,
            out_shape=[jax.ShapeDtypeStruct((n, d), F32)] + c_shapes,
            compiler_params=_params(), name="xattn",
        )(h, g, wq, kt, v, wo, *casts)
    g_ffn, w_router, tri = router
    return pl.pallas_call(
        _xattn_router_kernel, grid=(n // tm,),
        in_specs=in_specs + [_resident(g_ffn.shape), _resident(w_router.shape), _resident(tri.shape)],
        out_specs=[pl.BlockSpec((tm, d), row), pl.BlockSpec((tm * ROW_CHUNKS, V7X_LANES), row),
                   pl.BlockSpec((tm, V7X_LANES), row),
                   pl.BlockSpec((V7X_SUBLANES, tm), lambda i: (0, i)),
                   pl.BlockSpec((V7X_SUBLANES, V7X_LANES), lambda i: (0, 0))],
        out_shape=[jax.ShapeDtypeStruct((n, d), F32),
                   jax.ShapeDtypeStruct((n * ROW_CHUNKS, V7X_LANES), F32),
                   jax.ShapeDtypeStruct((n, V7X_LANES), F32),
                   jax.ShapeDtypeStruct((V7X_SUBLANES, n), F32),
                   jax.ShapeDtypeStruct((V7X_SUBLANES, V7X_LANES), F32)],
        scratch_shapes=[pltpu.VMEM((1, V7X_LANES), F32)],
        compiler_params=_params(), name="xattn_router",
    )(h, g, wq, kt, v, wo, g_ffn, w_router, tri)


def _swiglu(hn, wg_ref, wu_ref, wd_ref):
    acc = None
    for start, size in FF_CHUNKS:
        gate = _dot(hn, wg_ref[:, start:start + size])
        up = _dot(hn, wu_ref[:, start:start + size])
        act = (gate * _sigmoid(gate) * up).astype(BF16)
        part = _dot(act, wd_ref[start:start + size, :])
        acc = part if acc is None else acc + part
    return acc


def _ffn_kernel(h_ref, g_ref, wg_ref, wu_ref, wd_ref, o_ref):
    h = h_ref[...]
    hn = _rms(h, g_ref[...]).astype(BF16)
    o_ref[...] = h + _swiglu(hn, wg_ref, wu_ref, wd_ref)


def _ffn(h, g, wg, wu, wd, casts):
    n, d = h.shape
    tm = TOKEN_TILE
    row = lambda i: (i, 0)
    c_in, c_out, c_shapes = _cast_specs(casts, n // tm)
    return pl.pallas_call(
        _with_casts(_ffn_kernel, 5, 1, len(casts)), grid=(n // tm,),
        in_specs=[pl.BlockSpec((tm, d), row), _resident(g.shape), _resident(wg.shape),
                  _resident(wu.shape), _resident(wd.shape)] + c_in,
        out_specs=[pl.BlockSpec((tm, d), row)] + c_out,
        out_shape=[jax.ShapeDtypeStruct((n, d), F32)] + c_shapes,
        compiler_params=_params(), name="ffn",
    )(h, g, wg, wu, wd, *casts)


def _gmlp_kernel(h_ref, g_ref, w_in_ref, lng_ref, lnb_ref, ws_ref, bias_ref, w_out_ref, o_ref):
    tm = h_ref.shape[0]
    nblk = tm // C_BLOCK
    h = h_ref[...]
    hn = _rms(h, g_ref[...]).astype(BF16)
    u = _gelu_tanh(_dot(hn, w_in_ref[:, 0:C_WIDTH]))
    v = _gelu_tanh(_dot(hn, w_in_ref[:, C_WIDTH:2 * C_WIDTH]))
    v = _layer_norm(v, lng_ref[...], lnb_ref[...]).astype(BF16)

    qi = lax.broadcasted_iota(I32, (C_BLOCK, C_BLOCK), 0) // CHUNK
    kj = lax.broadcasted_iota(I32, (C_BLOCK, C_BLOCK), 1) // CHUNK
    causal = kj <= qi
    mixed = []
    for grp in range(C_GROUPS):
        lo, hi = grp * C_GROUP_DIM, (grp + 1) * C_GROUP_DIM
        w = jnp.where(causal, ws_ref[grp], 0.0).astype(BF16)
        rhs = jnp.concatenate([v[b * C_BLOCK:(b + 1) * C_BLOCK, lo:hi] for b in range(nblk)], axis=1)
        mixed.append(_dot(w, rhs))
    rows = []
    for b in range(nblk):
        blk = jnp.concatenate([m[:, b * C_GROUP_DIM:(b + 1) * C_GROUP_DIM] for m in mixed], axis=1)
        rows.append(blk + bias_ref[...])
    s = jnp.concatenate(rows, axis=0)
    o_ref[...] = h + _dot((u * s).astype(BF16), w_out_ref[...])


def _gmlp(h, g, w_in, lng, lnb, ws, bias, w_out):
    n, d = h.shape
    tm = WIDE_TOKEN_TILE
    row = lambda i: (i, 0)
    return pl.pallas_call(
        _gmlp_kernel, grid=(n // tm,),
        in_specs=[pl.BlockSpec((tm, d), row), _resident(g.shape), _resident(w_in.shape),
                  _resident(lng.shape), _resident(lnb.shape), _resident(ws.shape),
                  _resident(bias.shape), _resident(w_out.shape)],
        out_specs=pl.BlockSpec((tm, d), row),
        out_shape=jax.ShapeDtypeStruct((n, d), F32),
        compiler_params=_params(), name="gmlp",
    )(h, g, w_in, lng, lnb, ws, bias, w_out)


def _row_tile(ref, r):
    return ref.at[pl.ds(pl.multiple_of(r * ROW_CHUNKS, ROW_CHUNKS), ROW_CHUNKS), :]


def _gather_rows(src_hbm, rows_ref, n_rows, dst, sem):
    for i in range(n_rows):
        r = rows_ref[i // V7X_LANES, i % V7X_LANES]
        pltpu.make_async_copy(_row_tile(src_hbm, r), dst.at[i * ROW_CHUNKS:(i + 1) * ROW_CHUNKS, :],
                              sem).start(priority=i % DMA_PRIORITIES)


def _wait_rows(src_hbm, dst, sem):
    pltpu.make_async_copy(src_hbm.at[pl.ds(0, dst.shape[0]), :], dst, sem).wait()


def _expert_kernel(tile_expert_ref, n_tiles_ref, nxt_ref, cur_ref, hn_hbm, wg_ref, wu_ref, wd_ref,
                   y_ref, xbuf0, xbuf1, gsem):
    j = pl.program_id(0)
    n_tiles = n_tiles_ref[0]
    tm = TOKEN_TILE
    xbufs = (xbuf0, xbuf1)

    @pl.when(j == 0)
    def _():
        _gather_rows(hn_hbm, cur_ref, tm, xbuf0, gsem.at[0])

    def tile(slot):
        _wait_rows(hn_hbm, xbufs[slot], gsem.at[slot])
        x = _load_row_tiles(xbufs[slot], 0, tm).astype(BF16)
        _gather_rows(hn_hbm, nxt_ref, tm, xbufs[1 - slot], gsem.at[1 - slot])
        _store_row_tiles(y_ref, _swiglu(x, wg_ref, wu_ref, wd_ref))

        @pl.when(j == n_tiles - 1)
        def _():
            _wait_rows(hn_hbm, xbufs[1 - slot], gsem.at[1 - slot])

    for slot in range(2):
        pl.when((j < n_tiles) & (j % 2 == slot))(functools.partial(tile, slot))

    @pl.when(j >= n_tiles)
    def _():
        y_ref[...] = jnp.zeros(y_ref.shape, F32)


def _experts(hn_tiles, tile_tokens, tile_expert, n_tiles, wg, wu, wd):
    tm = TOKEN_TILE
    max_tiles = tile_expert.shape[0]
    code_rows = tm // V7X_LANES
    expert = lambda j, te, nt: (te[j], 0, 0)
    grid_spec = pltpu.PrefetchScalarGridSpec(
        num_scalar_prefetch=2, grid=(max_tiles,),
        in_specs=[pl.BlockSpec((None, code_rows, V7X_LANES),
                               lambda j, te, nt: (jnp.minimum(j + 1, max_tiles - 1), 0, 0),
                               memory_space=pltpu.SMEM),
                  pl.BlockSpec((None, code_rows, V7X_LANES), lambda j, te, nt: (j, 0, 0),
                               memory_space=pltpu.SMEM),
                  pl.BlockSpec(memory_space=pl.ANY),
                  pl.BlockSpec((None, D_MODEL, D_FF), expert),
                  pl.BlockSpec((None, D_MODEL, D_FF), expert),
                  pl.BlockSpec((None, D_FF, D_MODEL), expert)],
        out_specs=pl.BlockSpec((tm * ROW_CHUNKS, V7X_LANES), lambda j, te, nt: (j, 0)),
        scratch_shapes=[pltpu.VMEM((tm * ROW_CHUNKS, V7X_LANES), F32),
                        pltpu.VMEM((tm * ROW_CHUNKS, V7X_LANES), F32),
                        pltpu.SemaphoreType.DMA((2,))])
    return pl.pallas_call(
        _expert_kernel, grid_spec=grid_spec,
        out_shape=jax.ShapeDtypeStruct((max_tiles * tm * ROW_CHUNKS, V7X_LANES), F32),
        compiler_params=_params(), name="experts",
    )(tile_expert, n_tiles, tile_tokens, tile_tokens, hn_tiles, wg, wu, wd)


def _tile_plan(route_t, counts, n_tokens):
    tm = TOKEN_TILE
    max_tiles = TOP_K * n_tokens // tm + N_EXPERTS
    counts = counts.astype(I32)
    tiles_e = (counts + tm - 1) // tm
    tile_end = jnp.cumsum(tiles_e)
    row_start = (tile_end - tiles_e) * tm
    n_tiles = tile_end[-1:]
    tile_ids = jnp.arange(max_tiles, dtype=I32)
    tile_expert = jnp.minimum(jnp.sum(tile_end[None, :] <= tile_ids[:, None], axis=1),
                              N_EXPERTS - 1).astype(I32)

    def row_of(e_row, r_row):
        e = route_t[e_row].astype(I32)
        start = jnp.zeros_like(e)
        for k in range(N_EXPERTS):
            start = jnp.where(e == k, row_start[k], start)
        return start + route_t[r_row].astype(I32)

    pos1 = row_of(ROUTE_E1, ROUTE_R1)
    pos2 = row_of(ROUTE_E2, ROUTE_R2)
    tok = jnp.arange(n_tokens, dtype=I32)
    pad_j = jnp.arange(tm, dtype=I32)[None, :]
    pad_used = pad_j < (tiles_e * tm - counts)[:, None]
    pad_pos = jnp.where(pad_used, (row_start + counts)[:, None] + pad_j,
                        max_tiles * tm + jnp.arange(N_EXPERTS, dtype=I32)[:, None] * tm + pad_j)
    pad_tok = jnp.broadcast_to(pad_j, (N_EXPERTS, tm))
    keys = jnp.concatenate([pos1, pos2, pad_pos.reshape(-1)])
    vals = jnp.concatenate([tok, tok, pad_tok.reshape(-1)])
    tok_bits = n_tokens.bit_length() - 1
    assert (max_tiles + N_EXPERTS) * tm <= 1 << (32 - tok_bits)
    packed = jnp.sort((keys.astype(jnp.uint32) << tok_bits) | vals.astype(jnp.uint32))
    tile_tokens = (packed & (n_tokens - 1)).astype(I32)
    code_rows = tm // V7X_LANES
    token_rows = jnp.concatenate([pos1.reshape(n_tokens // tm, code_rows, V7X_LANES),
                                  pos2.reshape(n_tokens // tm, code_rows, V7X_LANES)], axis=1)
    return tile_tokens.reshape(max_tiles, code_rows, V7X_LANES), tile_expert, n_tiles, token_rows


def _combine_kernel(nxt_ref, cur_ref, h_ref, route_ref, y_hbm, gf_ref, o_ref, ybuf0, ybuf1, gsem):
    i = pl.program_id(0)
    tm = h_ref.shape[0]
    ybufs = (ybuf0, ybuf1)

    @pl.when(i == 0)
    def _():
        _gather_rows(y_hbm, cur_ref, TOP_K * tm, ybuf0, gsem.at[0])

    def tile(slot):
        _wait_rows(y_hbm, ybufs[slot], gsem.at[slot])

        @pl.when(i + 1 < pl.num_programs(0))
        def _():
            _gather_rows(y_hbm, nxt_ref, TOP_K * tm, ybufs[1 - slot], gsem.at[1 - slot])

        w1 = route_ref[:, ROUTE_W1:ROUTE_W1 + 1]
        w2 = route_ref[:, ROUTE_W2:ROUTE_W2 + 1]
        y = (w1 * _load_row_tiles(ybufs[slot], 0, tm)
             + w2 * _load_row_tiles(ybufs[slot], tm * ROW_CHUNKS, tm))
        o_ref[...] = _rms(h_ref[...] + y, gf_ref[...])

    for slot in range(2):
        pl.when(i % 2 == slot)(functools.partial(tile, slot))


def _combine(h, route, y_tiles, token_rows, g_final):
    n, d = h.shape
    tm = TOKEN_TILE
    steps = n // tm
    row = lambda i: (i, 0)
    rows_block = (None,) + token_rows.shape[1:]
    return pl.pallas_call(
        _combine_kernel, grid=(steps,),
        in_specs=[pl.BlockSpec(rows_block, lambda i: (jnp.minimum(i + 1, steps - 1), 0, 0),
                               memory_space=pltpu.SMEM),
                  pl.BlockSpec(rows_block, lambda i: (i, 0, 0), memory_space=pltpu.SMEM),
                  pl.BlockSpec((tm, d), row), pl.BlockSpec((tm, V7X_LANES), row),
                  pl.BlockSpec(memory_space=pl.ANY), _resident(g_final.shape)],
        out_specs=pl.BlockSpec((tm, d), row),
        out_shape=jax.ShapeDtypeStruct((n, d), F32),
        scratch_shapes=[pltpu.VMEM((TOP_K * tm * ROW_CHUNKS, V7X_LANES), F32),
                        pltpu.VMEM((TOP_K * tm * ROW_CHUNKS, V7X_LANES), F32),
                        pltpu.SemaphoreType.DMA((2,))],
        compiler_params=_params(), name="combine",
    )(token_rows, token_rows, h, route, y_tiles, g_final)


def kernel(x, mem, norm_mix_g, norm_xattn_g, norm_mem_g, norm_ffn_g, final_norm_g, xa_w_q, xa_w_k, xa_w_v, xa_w_o, cv_w_in, cv_a_conv_w, cv_a_conv_b, cv_a_ln_g, cv_a_ln_b, cv_b_conv_w, cv_w_out, ffn_w_gate, ffn_w_up, ffn_w_down, sg_w_in, sg_ln_g, sg_ln_b, sg_w_s, sg_b_s, sg_w_out, moe_w_router, moe_w_gate, moe_w_up, moe_w_down):
    batch, seq, d = x.shape
    mem_len = mem.shape[1]
    n = batch * seq
    assert n & (n - 1) == 0 and seq % WIDE_TOKEN_TILE == 0 and WIDE_TOKEN_TILE % TOKEN_TILE == 0
    vec = lambda a: a.reshape(1, -1)
    bf = lambda a: a.astype(BF16)

    h = x.reshape(n, d)
    mem2d = mem.reshape(batch * mem_len, d)

    h, ffn_wg, ffn_wu, ffn_wd = _conv_mixer(
        h, vec(norm_mix_g[0]), bf(cv_w_in[0]), cv_a_conv_w[0], vec(cv_a_conv_b[0]),
        vec(cv_a_ln_g[0]), vec(cv_a_ln_b[0]), cv_b_conv_w[0], bf(cv_w_out[0]), seq,
        casts=(ffn_w_gate[0], ffn_w_up[0], ffn_w_down[0]))
    kt, v = _mem_kv(mem2d, vec(norm_mem_g[0]), bf(xa_w_k[0]), bf(xa_w_v[0]), batch, mem_len)
    h, sg_in, sg_out, wq1, wk1, wv1, wo1 = _xattn(
        h, vec(norm_xattn_g[0]), bf(xa_w_q[0]), kt, v, bf(xa_w_o[0]), seq, mem_len,
        casts=(sg_w_in[0], sg_w_out[0], xa_w_q[1], xa_w_k[1], xa_w_v[1], xa_w_o[1]))
    flat = lambda w: w.reshape(-1, w.shape[-1])
    h, moe_wg, moe_wu, moe_wd = _ffn(
        h, vec(norm_ffn_g[0]), ffn_wg, ffn_wu, ffn_wd,
        casts=(flat(moe_w_gate[0]), flat(moe_w_up[0]), flat(moe_w_down[0])))

    bias = jnp.repeat(sg_b_s[0].T, C_GROUP_DIM, axis=1)
    h = _gmlp(h, vec(norm_mix_g[1]), sg_in, vec(sg_ln_g[0]), vec(sg_ln_b[0]), sg_w_s[0], bias, sg_out)
    kt, v = _mem_kv(mem2d, vec(norm_mem_g[1]), wk1, wv1, batch, mem_len)
    w_r = moe_w_router[0]
    w_r_hi = bf(w_r)
    w_r_lo = bf(w_r - w_r_hi.astype(F32))
    w_router = jnp.pad(jnp.concatenate([w_r_hi, w_r_lo], axis=1),
                       ((0, 0), (0, V7X_LANES - 2 * N_EXPERTS)))
    tri = jnp.tri(TOKEN_TILE, k=-1, dtype=BF16)
    h, hn_tiles, route, route_t, counts = _xattn(
        h, vec(norm_xattn_g[1]), wq1, kt, v, wo1, seq, mem_len,
        router=(vec(norm_ffn_g[1]), w_router, tri))
    tile_tokens, tile_expert, n_tiles, token_rows = _tile_plan(route_t, counts[0, :N_EXPERTS], n)
    y_tiles = _experts(hn_tiles, tile_tokens, tile_expert, n_tiles,
                       moe_wg.reshape(moe_w_gate[0].shape), moe_wu.reshape(moe_w_up[0].shape),
                       moe_wd.reshape(moe_w_down[0].shape))
    out = _combine(h, route, y_tiles, token_rows, vec(final_norm_g))
    return out.reshape(batch, seq, d)
```

```python
import functools

import jax
import jax.numpy as jnp
from jax import lax
from jax.experimental import pallas as pl
from jax.experimental.pallas import tpu as pltpu

F32 = jnp.float32
BF16 = jnp.bfloat16
I32 = jnp.int32

D_MODEL = 1024
A_WIDTH = 512
A_TAPS = 31
B_WIDTH = 512
B_TAPS = 3
C_WIDTH = 1024
C_GROUPS = 8
C_GROUP_DIM = 128
C_BLOCK = 128
CHUNK = 64
XA_HEADS = 4
XA_HEAD_DIM = 256
D_FF = 2816
N_EXPERTS = 8
TOP_K = 2
RMS_EPS = 1e-6
LN_EPS = 1e-5

V7X_LANES = 128
V7X_SUBLANES = 8
BF16_SUBLANES = 16
V7X_VMEM_LIMIT_BYTES = 60000 * 1024
DMA_PRIORITIES = 2

TOKEN_TILE = 512
WIDE_TOKEN_TILE = 1024
CONV_HALO = 32
CONV_ROW_CHUNK = 32
FF_CHUNKS = ((0, 512), (512, 512), (1024, 512), (1536, 512), (2048, 512), (2560, 256))
ROW_CHUNKS = D_MODEL // V7X_LANES
assert ROW_CHUNKS == V7X_SUBLANES
ROUTE_W1, ROUTE_W2, ROUTE_E1, ROUTE_E2, ROUTE_R1, ROUTE_R2 = range(6)


def _params(n_axes=1):
    return pltpu.CompilerParams(dimension_semantics=("arbitrary",) * n_axes,
                                vmem_limit_bytes=V7X_VMEM_LIMIT_BYTES)


def _resident(shape):
    return pl.BlockSpec(shape, lambda *_: (0,) * len(shape), pipeline_mode=pl.Buffered(1))


def _rms(x, g):
    return x * lax.rsqrt(jnp.mean(x * x, axis=-1, keepdims=True) + RMS_EPS) * g


def _layer_norm(x, g, b):
    mu = jnp.mean(x, axis=-1, keepdims=True)
    xc = x - mu
    var = jnp.mean(xc * xc, axis=-1, keepdims=True)
    return xc * lax.rsqrt(var + LN_EPS) * g + b


def _sigmoid(x):
    return 1.0 / (1.0 + jnp.exp(-x))


def _gelu_tanh(x):
    return 0.5 * x * (1.0 + jnp.tanh(0.7978845608028654 * (x + 0.044715 * (x * x * x))))


def _dot(a, b):
    return jnp.dot(a, b, preferred_element_type=F32)


def _store_row_tiles(ref, x):
    rows = x.shape[0]
    for s in range(ROW_CHUNKS):
        ref[pl.ds(s, rows, stride=ROW_CHUNKS), :] = x[:, s * V7X_LANES:(s + 1) * V7X_LANES]


def _load_row_tiles(ref, base, rows):
    return jnp.concatenate(
        [ref[pl.ds(base + s, rows, stride=ROW_CHUNKS), :] for s in range(ROW_CHUNKS)], axis=1)


def _with_casts(body, n_in, n_out, n_cast):
    def kernel(*refs):
        a, b, c = n_in + n_cast, n_in + n_cast + n_out, n_in + 2 * n_cast + n_out
        body(*refs[:n_in], *refs[a:b], *refs[c:])
        for src_ref, dst_ref in zip(refs[n_in:a], refs[b:c]):
            dst_ref[...] = src_ref[...].astype(BF16)
    return kernel


def _cast_specs(weights, steps):
    in_specs, out_specs, out_shapes = [], [], []
    for w in weights:
        rows, cols = w.shape
        block = next(r for r in range(BF16_SUBLANES, rows + 1, BF16_SUBLANES)
                     if rows % r == 0 and r * steps >= rows)
        last = rows // block - 1
        spec = pl.BlockSpec((block, cols), lambda i, last=last: (jnp.minimum(i, last), 0))
        in_specs.append(spec)
        out_specs.append(spec)
        out_shapes.append(jax.ShapeDtypeStruct(w.shape, BF16))
    return in_specs, out_specs, out_shapes


def _conv_mixer_kernel(tiles_per_batch, x_ref, g_ref, w_in_ref, aw_ref, ab_ref, lng_ref, lnb_ref,
                       bw_ref, w_out_ref, o_ref, ext_ref, bext_ref, apost_ref):
    tm = x_ref.shape[0]
    a_tiles = A_WIDTH // V7X_LANES
    b_tiles = B_WIDTH // V7X_LANES
    lanes = lambda t: slice(t * V7X_LANES, (t + 1) * V7X_LANES)

    @pl.when(pl.program_id(0) % tiles_per_batch == 0)
    def _():
        ext_ref[:, 0:CONV_HALO, :] = jnp.zeros((a_tiles, CONV_HALO, V7X_LANES), F32)
        bext_ref[:, 0:V7X_SUBLANES, :] = jnp.zeros((b_tiles, V7X_SUBLANES, V7X_LANES), F32)

    x = x_ref[...]
    hn = _rms(x, g_ref[...]).astype(BF16)

    a_val = _dot(hn, w_in_ref[:, 0:A_WIDTH])
    a_gate = _dot(hn, w_in_ref[:, A_WIDTH:2 * A_WIDTH])
    a = a_val * _sigmoid(a_gate)
    for t in range(a_tiles):
        ext_ref[t, CONV_HALO:CONV_HALO + tm, :] = a[:, lanes(t)]

    base = 2 * A_WIDTH
    g_b = _dot(hn, w_in_ref[:, base:base + B_WIDTH])
    g_c = _dot(hn, w_in_ref[:, base + B_WIDTH:base + 2 * B_WIDTH])
    h_b = _dot(hn, w_in_ref[:, base + 2 * B_WIDTH:base + 3 * B_WIDTH])
    u = g_c * h_b
    conv_b = []
    for t in range(b_tiles):
        bext_ref[t, V7X_SUBLANES:V7X_SUBLANES + tm, :] = u[:, lanes(t)]
        acc = jnp.zeros((tm, V7X_LANES), F32)
        for k in range(B_TAPS):
            off = V7X_SUBLANES - (B_TAPS - 1) + k
            acc = acc + bw_ref[k:k + 1, lanes(t)] * bext_ref[t, off:off + tm, :]
        conv_b.append(acc)
        bext_ref[t, 0:V7X_SUBLANES, :] = bext_ref[t, tm:tm + V7X_SUBLANES, :]
    b_out = (g_b * jnp.concatenate(conv_b, axis=1)).astype(BF16)

    first = CONV_HALO - (A_TAPS - 1)
    for r0 in range(0, tm, CONV_ROW_CHUNK):
        parts = []
        for t in range(a_tiles):
            acc = jnp.zeros((CONV_ROW_CHUNK, V7X_LANES), F32) + ab_ref[:, lanes(t)]
            for k in range(A_TAPS):
                start = r0 + first + k
                acc = acc + aw_ref[k:k + 1, lanes(t)] * ext_ref[t, start:start + CONV_ROW_CHUNK, :]
            parts.append(acc)
        y = _layer_norm(jnp.concatenate(parts, axis=1), lng_ref[...], lnb_ref[...])
        apost_ref[r0:r0 + CONV_ROW_CHUNK, :] = (y * _sigmoid(y)).astype(BF16)
    for t in range(a_tiles):
        ext_ref[t, 0:CONV_HALO, :] = ext_ref[t, tm:tm + CONV_HALO, :]

    o_ref[...] = (x + _dot(apost_ref[...], w_out_ref[0:A_WIDTH, :])
                  + _dot(b_out, w_out_ref[A_WIDTH:A_WIDTH + B_WIDTH, :]))


def _conv_mixer(x, g, w_in, aw, ab, lng, lnb, bw, w_out, seq, casts):
    n, d = x.shape
    tm = WIDE_TOKEN_TILE
    row = lambda i: (i, 0)
    c_in, c_out, c_shapes = _cast_specs(casts, n // tm)
    body = functools.partial(_conv_mixer_kernel, seq // tm)
    return pl.pallas_call(
        _with_casts(body, 9, 1, len(casts)),
        grid=(n // tm,),
        in_specs=[pl.BlockSpec((tm, d), row), _resident(g.shape), _resident(w_in.shape),
                  _resident(aw.shape), _resident(ab.shape), _resident(lng.shape), _resident(lnb.shape),
                  _resident(bw.shape), _resident(w_out.shape)] + c_in,
        out_specs=[pl.BlockSpec((tm, d), row)] + c_out,
        out_shape=[jax.ShapeDtypeStruct((n, d), F32)] + c_shapes,
        scratch_shapes=[pltpu.VMEM((A_WIDTH // V7X_LANES, tm + CONV_HALO, V7X_LANES), F32),
                        pltpu.VMEM((B_WIDTH // V7X_LANES, tm + V7X_SUBLANES, V7X_LANES), F32),
                        pltpu.VMEM((tm, A_WIDTH), BF16)],
        compiler_params=_params(),
        name="conv_mixer",
    )(x, g, w_in, aw, ab, lng, lnb, bw, w_out, *casts)


def _mem_kv_kernel(mem_ref, g_ref, wk_ref, wv_ref, kt_ref, v_ref):
    mn = _rms(mem_ref[...], g_ref[...]).astype(BF16)
    kt_ref[...] = _dot(mn, wk_ref[...]).T.astype(BF16)
    v_ref[...] = _dot(mn, wv_ref[...]).astype(BF16)


def _mem_kv(mem2d, g, wk, wv, batch, mem_len):
    d = mem2d.shape[1]
    return pl.pallas_call(
        _mem_kv_kernel,
        grid=(batch,),
        in_specs=[pl.BlockSpec((mem_len, d), lambda b: (b, 0)), _resident(g.shape),
                  _resident(wk.shape), _resident(wv.shape)],
        out_specs=[pl.BlockSpec((d, mem_len), lambda b: (b, 0)),
                   pl.BlockSpec((mem_len, d), lambda b: (b, 0))],
        out_shape=[jax.ShapeDtypeStruct((batch * d, mem_len), BF16),
                   jax.ShapeDtypeStruct((batch * mem_len, d), BF16)],
        compiler_params=_params(),
        name="mem_kv",
    )(mem2d, g, wk, wv)


def _attend(h, g_ref, wq_ref, kt_ref, v_ref, wo_ref):
    hn = _rms(h, g_ref[...]).astype(BF16)
    q = (_dot(hn, wq_ref[...]) * (XA_HEAD_DIM ** -0.5)).astype(BF16)
    heads = []
    for hd in range(XA_HEADS):
        lo, hi = hd * XA_HEAD_DIM, (hd + 1) * XA_HEAD_DIM
        s = _dot(q[:, lo:hi], kt_ref[lo:hi, :])
        p = jnp.exp(s - jnp.max(s, axis=-1, keepdims=True))
        p = p * (1.0 / jnp.sum(p, axis=-1, keepdims=True))
        heads.append(_dot(p.astype(BF16), v_ref[:, lo:hi]))
    o = jnp.concatenate(heads, axis=-1).astype(BF16)
    return h + _dot(o, wo_ref[...])


def _xattn_kernel(h_ref, g_ref, wq_ref, kt_ref, v_ref, wo_ref, o_ref):
    o_ref[...] = _attend(h_ref[...], g_ref, wq_ref, kt_ref, v_ref, wo_ref)


def _xattn_router_kernel(h_ref, g_ref, wq_ref, kt_ref, v_ref, wo_ref, gf_ref, wr_ref, tri_ref,
                         o_ref, hn_ref, route_ref, route_t_ref, counts_ref, seen_ref):
    @pl.when(pl.program_id(0) == 0)
    def _():
        seen_ref[...] = jnp.zeros(seen_ref.shape, F32)

    h = _attend(h_ref[...], g_ref, wq_ref, kt_ref, v_ref, wo_ref)
    o_ref[...] = h
    hn = _rms(h, gf_ref[...])
    _store_row_tiles(hn_ref, hn)

    hn_hi = hn.astype(BF16)
    hn_lo = (hn - hn_hi.astype(F32)).astype(BF16)
    p_hi = _dot(hn_hi, wr_ref[...])
    p_lo = _dot(hn_lo, wr_ref[...])
    logits = (pltpu.roll(p_hi, V7X_LANES - N_EXPERTS, axis=1) + p_lo) + p_hi

    lane = lax.broadcasted_iota(I32, logits.shape, 1)
    neg = jnp.float32(-jnp.inf)
    logits = jnp.where(lane < N_EXPERTS, logits, neg)
    m1 = jnp.max(logits, axis=-1, keepdims=True)
    e1 = jnp.min(jnp.where(logits == m1, lane, V7X_LANES), axis=-1, keepdims=True)
    rest = jnp.where(lane == e1, neg, logits)
    m2 = jnp.max(rest, axis=-1, keepdims=True)
    e2 = jnp.min(jnp.where(rest == m2, lane, V7X_LANES), axis=-1, keepdims=True)
    ex = jnp.exp(m2 - m1)
    w1 = 1.0 / (1.0 + ex)
    w2 = ex * w1

    chosen = (lane == e1) | (lane == e2)
    member = jnp.where(chosen, 1.0, 0.0)
    rank = _dot(tri_ref[...], member.astype(BF16)) + seen_ref[...]
    seen_ref[...] = seen_ref[...] + jnp.sum(member, axis=0, keepdims=True)
    counts_ref[...] = jnp.broadcast_to(seen_ref[...], counts_ref.shape)
    r1 = jnp.sum(jnp.where(lane == e1, rank, 0.0), axis=-1, keepdims=True)
    r2 = jnp.sum(jnp.where(lane == e2, rank, 0.0), axis=-1, keepdims=True)

    record = jnp.zeros(logits.shape, F32)
    for slot, val in ((ROUTE_W1, w1), (ROUTE_W2, w2), (ROUTE_E1, e1.astype(F32)),
                      (ROUTE_E2, e2.astype(F32)), (ROUTE_R1, r1), (ROUTE_R2, r2)):
        record = jnp.where(lane == slot, val, record)
    route_ref[...] = record
    route_t_ref[...] = record.T[0:V7X_SUBLANES, :]


def _xattn(h, g, wq, kt, v, wo, seq, mem_len, router=None, casts=()):
    n, d = h.shape
    tm = WIDE_TOKEN_TILE
    tpb = seq // tm
    row = lambda i: (i, 0)
    in_specs = [pl.BlockSpec((tm, d), row), _resident(g.shape), _resident(wq.shape),
                pl.BlockSpec((d, mem_len), lambda i: (i // tpb, 0)),
                pl.BlockSpec((mem_len, d), lambda i: (i // tpb, 0)),
                _resident(wo.shape)]
    if router is None:
        c_in, c_out, c_shapes = _cast_specs(casts, n // tm)
        return pl.pallas_call(
            _with_casts(_xattn_kernel, 6, 1, len(casts)), grid=(n // tm,), in_specs=in_specs + c_in,
            out_specs=[pl.BlockSpec((tm, d), row)] + c_out,
            out_shape=[jax.ShapeDtypeStruct((n, d), F32)] + c_shapes,
            compiler_params=_params(), name="xattn",
        )(h, g, wq, kt, v, wo, *casts)
    g_ffn, w_router, tri = router
    return pl.pallas_call(
        _xattn_router_kernel, grid=(n // tm,),
        in_specs=in_specs + [_resident(g_ffn.shape), _resident(w_router.shape), _resident(tri.shape)],
        out_specs=[pl.BlockSpec((tm, d), row), pl.BlockSpec((tm * ROW_CHUNKS, V7X_LANES), row),
                   pl.BlockSpec((tm, V7X_LANES), row),
                   pl.BlockSpec((V7X_SUBLANES, tm), lambda i: (0, i)),
                   pl.BlockSpec((V7X_SUBLANES, V7X_LANES), lambda i: (0, 0))],
        out_shape=[jax.ShapeDtypeStruct((n, d), F32),
                   jax.ShapeDtypeStruct((n * ROW_CHUNKS, V7X_LANES), F32),
                   jax.ShapeDtypeStruct((n, V7X_LANES), F32),
                   jax.ShapeDtypeStruct((V7X_SUBLANES, n), F32),
                   jax.ShapeDtypeStruct((V7X_SUBLANES, V7X_LANES), F32)],
        scratch_shapes=[pltpu.VMEM((1, V7X_LANES), F32)],
        compiler_params=_params(), name="xattn_router",
    )(h, g, wq, kt, v, wo, g_ffn, w_router, tri)


def _swiglu(hn, wg_ref, wu_ref, wd_ref):
    acc = None
    for start, size in FF_CHUNKS:
        gate = _dot(hn, wg_ref[:, start:start + size])
        up = _dot(hn, wu_ref[:, start:start + size])
        act = (gate * _sigmoid(gate) * up).astype(BF16)
        part = _dot(act, wd_ref[start:start + size, :])
        acc = part if acc is None else acc + part
    return acc


def _ffn_kernel(h_ref, g_ref, wg_ref, wu_ref, wd_ref, o_ref):
    h = h_ref[...]
    hn = _rms(h, g_ref[...]).astype(BF16)
    o_ref[...] = h + _swiglu(hn, wg_ref, wu_ref, wd_ref)


def _ffn(h, g, wg, wu, wd, casts):
    n, d = h.shape
    tm = TOKEN_TILE
    row = lambda i: (i, 0)
    c_in, c_out, c_shapes = _cast_specs(casts, n // tm)
    return pl.pallas_call(
        _with_casts(_ffn_kernel, 5, 1, len(casts)), grid=(n // tm,),
        in_specs=[pl.BlockSpec((tm, d), row), _resident(g.shape), _resident(wg.shape),
                  _resident(wu.shape), _resident(wd.shape)] + c_in,
        out_specs=[pl.BlockSpec((tm, d), row)] + c_out,
        out_shape=[jax.ShapeDtypeStruct((n, d), F32)] + c_shapes,
        compiler_params=_params(), name="ffn",
    )(h, g, wg, wu, wd, *casts)


def _gmlp_kernel(h_ref, g_ref, w_in_ref, lng_ref, lnb_ref, ws_ref, bias_ref, w_out_ref, o_ref):
    tm = h_ref.shape[0]
    nblk = tm // C_BLOCK
    h = h_ref[...]
    hn = _rms(h, g_ref[...]).astype(BF16)
    u = _gelu_tanh(_dot(hn, w_in_ref[:, 0:C_WIDTH]))
    v = _gelu_tanh(_dot(hn, w_in_ref[:, C_WIDTH:2 * C_WIDTH]))
    v = _layer_norm(v, lng_ref[...], lnb_ref[...]).astype(BF16)

    qi = lax.broadcasted_iota(I32, (C_BLOCK, C_BLOCK), 0) // CHUNK
    kj = lax.broadcasted_iota(I32, (C_BLOCK, C_BLOCK), 1) // CHUNK
    causal = kj <= qi
    mixed = []
    for grp in range(C_GROUPS):
        lo, hi = grp * C_GROUP_DIM, (grp + 1) * C_GROUP_DIM
        w = jnp.where(causal, ws_ref[grp], 0.0).astype(BF16)
        rhs = jnp.concatenate([v[b * C_BLOCK:(b + 1) * C_BLOCK, lo:hi] for b in range(nblk)], axis=1)
        mixed.append(_dot(w, rhs))
    rows = []
    for b in range(nblk):
        blk = jnp.concatenate([m[:, b * C_GROUP_DIM:(b + 1) * C_GROUP_DIM] for m in mixed], axis=1)
        rows.append(blk + bias_ref[...])
    s = jnp.concatenate(rows, axis=0)
    o_ref[...] = h + _dot((u * s).astype(BF16), w_out_ref[...])


def _gmlp(h, g, w_in, lng, lnb, ws, bias, w_out):
    n, d = h.shape
    tm = WIDE_TOKEN_TILE
    row = lambda i: (i, 0)
    return pl.pallas_call(
        _gmlp_kernel, grid=(n // tm,),
        in_specs=[pl.BlockSpec((tm, d), row), _resident(g.shape), _resident(w_in.shape),
                  _resident(lng.shape), _resident(lnb.shape), _resident(ws.shape),
                  _resident(bias.shape), _resident(w_out.shape)],
        out_specs=pl.BlockSpec((tm, d), row),
        out_shape=jax.ShapeDtypeStruct((n, d), F32),
        compiler_params=_params(), name="gmlp",
    )(h, g, w_in, lng, lnb, ws, bias, w_out)


def _row_tile(ref, r):
    return ref.at[pl.ds(pl.multiple_of(r * ROW_CHUNKS, ROW_CHUNKS), ROW_CHUNKS), :]


def _gather_rows(src_hbm, rows_ref, n_rows, dst, sem):
    for i in range(n_rows):
        r = rows_ref[i // V7X_LANES, i % V7X_LANES]
        pltpu.make_async_copy(_row_tile(src_hbm, r), dst.at[i * ROW_CHUNKS:(i + 1) * ROW_CHUNKS, :],
                              sem).start(priority=i % DMA_PRIORITIES)


def _wait_rows(src_hbm, dst, sem):
    pltpu.make_async_copy(src_hbm.at[pl.ds(0, dst.shape[0]), :], dst, sem).wait()


def _expert_kernel(tile_expert_ref, n_tiles_ref, nxt_ref, cur_ref, hn_hbm, wg_ref, wu_ref, wd_ref,
                   y_ref, xbuf0, xbuf1, gsem):
    j = pl.program_id(0)
    n_tiles = n_tiles_ref[0]
    tm = TOKEN_TILE
    xbufs = (xbuf0, xbuf1)

    @pl.when(j == 0)
    def _():
        _gather_rows(hn_hbm, cur_ref, tm, xbuf0, gsem.at[0])

    def tile(slot):
        _wait_rows(hn_hbm, xbufs[slot], gsem.at[slot])
        x = _load_row_tiles(xbufs[slot], 0, tm).astype(BF16)
        _gather_rows(hn_hbm, nxt_ref, tm, xbufs[1 - slot], gsem.at[1 - slot])
        _store_row_tiles(y_ref, _swiglu(x, wg_ref, wu_ref, wd_ref))

        @pl.when(j == n_tiles - 1)
        def _():
            _wait_rows(hn_hbm, xbufs[1 - slot], gsem.at[1 - slot])

    for slot in range(2):
        pl.when((j < n_tiles) & (j % 2 == slot))(functools.partial(tile, slot))

    @pl.when(j >= n_tiles)
    def _():
        y_ref[...] = jnp.zeros(y_ref.shape, F32)


def _experts(hn_tiles, tile_tokens, tile_expert, n_tiles, wg, wu, wd):
    tm = TOKEN_TILE
    max_tiles = tile_expert.shape[0]
    code_rows = tm // V7X_LANES
    expert = lambda j, te, nt: (te[j], 0, 0)
    grid_spec = pltpu.PrefetchScalarGridSpec(
        num_scalar_prefetch=2, grid=(max_tiles,),
        in_specs=[pl.BlockSpec((None, code_rows, V7X_LANES),
                               lambda j, te, nt: (jnp.minimum(j + 1, max_tiles - 1), 0, 0),
                               memory_space=pltpu.SMEM),
                  pl.BlockSpec((None, code_rows, V7X_LANES), lambda j, te, nt: (j, 0, 0),
                               memory_space=pltpu.SMEM),
                  pl.BlockSpec(memory_space=pl.ANY),
                  pl.BlockSpec((None, D_MODEL, D_FF), expert),
                  pl.BlockSpec((None, D_MODEL, D_FF), expert),
                  pl.BlockSpec((None, D_FF, D_MODEL), expert)],
        out_specs=pl.BlockSpec((tm * ROW_CHUNKS, V7X_LANES), lambda j, te, nt: (j, 0)),
        scratch_shapes=[pltpu.VMEM((tm * ROW_CHUNKS, V7X_LANES), F32),
                        pltpu.VMEM((tm * ROW_CHUNKS, V7X_LANES), F32),
                        pltpu.SemaphoreType.DMA((2,))])
    return pl.pallas_call(
        _expert_kernel, grid_spec=grid_spec,
        out_shape=jax.ShapeDtypeStruct((max_tiles * tm * ROW_CHUNKS, V7X_LANES), F32),
        compiler_params=_params(), name="experts",
    )(tile_expert, n_tiles, tile_tokens, tile_tokens, hn_tiles, wg, wu, wd)


def _tile_plan(route_t, counts, n_tokens):
    tm = TOKEN_TILE
    max_tiles = TOP_K * n_tokens // tm + N_EXPERTS
    counts = counts.astype(I32)
    tiles_e = (counts + tm - 1) // tm
    tile_end = jnp.cumsum(tiles_e)
    row_start = (tile_end - tiles_e) * tm
    n_tiles = tile_end[-1:]
    tile_ids = jnp.arange(max_tiles, dtype=I32)
    tile_expert = jnp.minimum(jnp.sum(tile_end[None, :] <= tile_ids[:, None], axis=1),
                              N_EXPERTS - 1).astype(I32)

    def row_of(e_row, r_row):
        e = route_t[e_row].astype(I32)
        start = jnp.zeros_like(e)
        for k in range(N_EXPERTS):
            start = jnp.where(e == k, row_start[k], start)
        return start + route_t[r_row].astype(I32)

    pos1 = row_of(ROUTE_E1, ROUTE_R1)
    pos2 = row_of(ROUTE_E2, ROUTE_R2)
    tok = jnp.arange(n_tokens, dtype=I32)
    pad_j = jnp.arange(tm, dtype=I32)[None, :]
    pad_used = pad_j < (tiles_e * tm - counts)[:, None]
    pad_pos = jnp.where(pad_used, (row_start + counts)[:, None] + pad_j,
                        max_tiles * tm + jnp.arange(N_EXPERTS, dtype=I32)[:, None] * tm + pad_j)
    pad_tok = jnp.broadcast_to(pad_j, (N_EXPERTS, tm))
    keys = jnp.concatenate([pos1, pos2, pad_pos.reshape(-1)])
    vals = jnp.concatenate([tok, tok, pad_tok.reshape(-1)])
    tok_bits = n_tokens.bit_length() - 1
    assert (max_tiles + N_EXPERTS) * tm <= 1 << (32 - tok_bits)
    packed = jnp.sort((keys.astype(jnp.uint32) << tok_bits) | vals.astype(jnp.uint32))
    tile_tokens = (packed & (n_tokens - 1)).astype(I32)
    code_rows = tm // V7X_LANES
    token_rows = jnp.concatenate([pos1.reshape(n_tokens // tm, code_rows, V7X_LANES),
                                  pos2.reshape(n_tokens // tm, code_rows, V7X_LANES)], axis=1)
    return tile_tokens.reshape(max_tiles, code_rows, V7X_LANES), tile_expert, n_tiles, token_rows


def _combine_kernel(nxt_ref, cur_ref, h_ref, route_ref, y_hbm, gf_ref, o_ref, ybuf0, ybuf1, gsem):
    i = pl.program_id(0)
    tm = h_ref.shape[0]
    ybufs = (ybuf0, ybuf1)

    @pl.when(i == 0)
    def _():
        _gather_rows(y_hbm, cur_ref, TOP_K * tm, ybuf0, gsem.at[0])

    def tile(slot):
        _wait_rows(y_hbm, ybufs[slot], gsem.at[slot])

        @pl.when(i + 1 < pl.num_programs(0))
        def _():
            _gather_rows(y_hbm, nxt_ref, TOP_K * tm, ybufs[1 - slot], gsem.at[1 - slot])

        w1 = route_ref[:, ROUTE_W1:ROUTE_W1 + 1]
        w2 = route_ref[:, ROUTE_W2:ROUTE_W2 + 1]
        y = (w1 * _load_row_tiles(ybufs[slot], 0, tm)
             + w2 * _load_row_tiles(ybufs[slot], tm * ROW_CHUNKS, tm))
        o_ref[...] = _rms(h_ref[...] + y, gf_ref[...])

    for slot in range(2):
        pl.when(i % 2 == slot)(functools.partial(tile, slot))


def _combine(h, route, y_tiles, token_rows, g_final):
    n, d = h.shape
    tm = TOKEN_TILE
    steps = n // tm
    row = lambda i: (i, 0)
    rows_block = (None,) + token_rows.shape[1:]
    return pl.pallas_call(
        _combine_kernel, grid=(steps,),
        in_specs=[pl.BlockSpec(rows_block, lambda i: (jnp.minimum(i + 1, steps - 1), 0, 0),
                               memory_space=pltpu.SMEM),
                  pl.BlockSpec(rows_block, lambda i: (i, 0, 0), memory_space=pltpu.SMEM),
                  pl.BlockSpec((tm, d), row), pl.BlockSpec((tm, V7X_LANES), row),
                  pl.BlockSpec(memory_space=pl.ANY), _resident(g_final.shape)],
        out_specs=pl.BlockSpec((tm, d), row),
        out_shape=jax.ShapeDtypeStruct((n, d), F32),
        scratch_shapes=[pltpu.VMEM((TOP_K * tm * ROW_CHUNKS, V7X_LANES), F32),
                        pltpu.VMEM((TOP_K * tm * ROW_CHUNKS, V7X_LANES), F32),
                        pltpu.SemaphoreType.DMA((2,))],
        compiler_params=_params(), name="combine",
    )(token_rows, token_rows, h, route, y_tiles, g_final)


def kernel(x, mem, norm_mix_g, norm_xattn_g, norm_mem_g, norm_ffn_g, final_norm_g, xa_w_q, xa_w_k, xa_w_v, xa_w_o, cv_w_in, cv_a_conv_w, cv_a_conv_b, cv_a_ln_g, cv_a_ln_b, cv_b_conv_w, cv_w_out, ffn_w_gate, ffn_w_up, ffn_w_down, sg_w_in, sg_ln_g, sg_ln_b, sg_w_s, sg_b_s, sg_w_out, moe_w_router, moe_w_gate, moe_w_up, moe_w_down):
    batch, seq, d = x.shape
    mem_len = mem.shape[1]
    n = batch * seq
    assert n & (n - 1) == 0 and seq % WIDE_TOKEN_TILE == 0 and WIDE_TOKEN_TILE % TOKEN_TILE == 0
    vec = lambda a: a.reshape(1, -1)
    bf = lambda a: a.astype(BF16)

    h = x.reshape(n, d)
    mem2d = mem.reshape(batch * mem_len, d)

    h, ffn_wg, ffn_wu, ffn_wd = _conv_mixer(
        h, vec(norm_mix_g[0]), bf(cv_w_in[0]), cv_a_conv_w[0], vec(cv_a_conv_b[0]),
        vec(cv_a_ln_g[0]), vec(cv_a_ln_b[0]), cv_b_conv_w[0], bf(cv_w_out[0]), seq,
        casts=(ffn_w_gate[0], ffn_w_up[0], ffn_w_down[0]))
    kt, v = _mem_kv(mem2d, vec(norm_mem_g[0]), bf(xa_w_k[0]), bf(xa_w_v[0]), batch, mem_len)
    h, sg_in, sg_out, wq1, wk1, wv1, wo1 = _xattn(
        h, vec(norm_xattn_g[0]), bf(xa_w_q[0]), kt, v, bf(xa_w_o[0]), seq, mem_len,
        casts=(sg_w_in[0], sg_w_out[0], xa_w_q[1], xa_w_k[1], xa_w_v[1], xa_w_o[1]))
    flat = lambda w: w.reshape(-1, w.shape[-1])
    h, moe_wg, moe_wu, moe_wd = _ffn(
        h, vec(norm_ffn_g[0]), ffn_wg, ffn_wu, ffn_wd,
        casts=(flat(moe_w_gate[0]), flat(moe_w_up[0]), flat(moe_w_down[0])))

    bias = jnp.repeat(sg_b_s[0].T, C_GROUP_DIM, axis=1)
    h = _gmlp(h, vec(norm_mix_g[1]), sg_in, vec(sg_ln_g[0]), vec(sg_ln_b[0]), sg_w_s[0], bias, sg_out)
    kt, v = _mem_kv(mem2d, vec(norm_mem_g[1]), wk1, wv1, batch, mem_len)
    w_r = moe_w_router[0]
    w_r_hi = bf(w_r)
    w_r_lo = bf(w_r - w_r_hi.astype(F32))
    w_router = jnp.pad(jnp.concatenate([w_r_hi, w_r_lo], axis=1),
                       ((0, 0), (0, V7X_LANES - 2 * N_EXPERTS)))
    tri = jnp.tri(WIDE_TOKEN_TILE, k=-1, dtype=BF16)
    h, hn_tiles, route, route_t, counts = _xattn(
        h, vec(norm_xattn_g[1]), wq1, kt, v, wo1, seq, mem_len,
        router=(vec(norm_ffn_g[1]), w_router, tri))
    tile_tokens, tile_expert, n_tiles, token_rows = _tile_plan(route_t, counts[0, :N_EXPERTS], n)
    y_tiles = _experts(hn_tiles, tile_tokens, tile_expert, n_tiles,
                       moe_wg.reshape(moe_w_gate[0].shape), moe_wu.reshape(moe_w_up[0].shape),
                       moe_wd.reshape(moe_w_down[0].shape))
    out = _combine(h, route, y_tiles, token_rows, vec(final_norm_g))
    return out.reshape(batch, seq, d)
```
